```python
import math, functools
import jax, jax.numpy as jnp
from jax import lax
import numpy as np

D_MODEL = 1024
BATCH = 8
SEQ = 2048
DEPTH = 1
DEC_BATCH = 32
DEC_SEQ = 4
PAST_LEN = 8192
PAGE_SIZE = 128

HEAD_DIM = 64
ATT_WIDTH = D_MODEL // 2
ATT_HEADS = ATT_WIDTH // HEAD_DIM
ATT_BRANCHES = ((128, 1), (512, 4), (2048, 16))
ATT_WINDOW = max(w for w, _ in ATT_BRANCHES)
ATT_SCALE = HEAD_DIM ** -0.5
HGRN_WIDTH = D_MODEL // 2
HGRN_DK = 128
HGRN_DV = 128
HGRN_HEADS = HGRN_WIDTH // HGRN_DK
HGRN_CHUNK = 64
MIX_WIDTH = ATT_WIDTH + HGRN_WIDTH
IN_SPLITS = (ATT_WIDTH, ATT_WIDTH, ATT_WIDTH, HGRN_WIDTH, HGRN_WIDTH, HGRN_WIDTH, HGRN_WIDTH)
IN_COLS = sum(IN_SPLITS)
N_KEYS = 128
N_EXPERTS = N_KEYS * N_KEYS
PEER_HEADS = 8
PEER_DKEY = 256
PEER_TOPK = 16
PEER_BLOCK = 128
EPS = 1e-6

kernel_name = "hybrid_dilated_attn_hgrn2_peer_step"


def rmsnorm(x, w):
    xf = x.astype(jnp.float32)
    y = xf * lax.rsqrt(jnp.mean(xf * xf, axis=-1, keepdims=True) + EPS)
    return (y * w.astype(jnp.float32)).astype(x.dtype)


def dilated_branch_prompt(q, k, v, dil, steps):
    B, S, H, Dh = q.shape
    L = S // dil
    nb = -(-L // steps)
    Lp = nb * steps

    def to_blocks(t):
        t = t.reshape(B, L, dil, H, Dh).transpose(0, 2, 1, 3, 4)
        t = jnp.pad(t, ((0, 0), (0, 0), (0, Lp - L), (0, 0), (0, 0)))
        return t.reshape(B, dil, nb, steps, H, Dh)

    def with_prev(t):
        prev = jnp.pad(t[:, :, :-1], ((0, 0), (0, 0), (1, 0), (0, 0), (0, 0), (0, 0)))
        return jnp.concatenate([prev, t], axis=3)

    qb, kb, vb = to_blocks(q), to_blocks(k), to_blocks(v)
    kw, vw = with_prev(kb), with_prev(vb)
    s = jnp.einsum('brnqhd,brnkhd->brnhqk', qb, kw, preferred_element_type=jnp.float32)
    a = jnp.arange(steps)[:, None]
    c = jnp.arange(2 * steps)[None, :]
    dist = steps + a - c
    blk = jnp.arange(nb)[:, None, None]
    valid = (dist >= 0) & (dist <= steps) & ((blk > 0) | (c >= steps))
    s = jnp.where(valid[:, None], s, -jnp.inf)
    m = jnp.max(s, axis=-1)
    p = jnp.exp(s - m[..., None])
    l = jnp.sum(p, axis=-1)
    acc = jnp.einsum('brnhqk,brnkhd->brnqhd', p, vw.astype(jnp.float32))

    def back(t):
        rest = t.shape[5:]
        t = t.reshape((B, dil, Lp, H) + rest)[:, :, :L]
        t = jnp.swapaxes(t, 1, 2)
        return t.reshape((B, S, H) + rest)

    return (back(jnp.swapaxes(m, 3, 4)), back(jnp.swapaxes(l, 3, 4)), back(acc))


def dilated_branch_sample(q, kk, vv, dil, steps, n_buf):
    T = q.shape[1]
    idx = n_buf + jnp.arange(T)[:, None] - jnp.arange(steps + 1)[None, :] * dil
    valid = idx >= 0
    idxc = jnp.maximum(idx, 0)
    kg = kk[:, idxc]
    vg = vv[:, idxc]
    s = jnp.einsum('bthd,btjhd->bthj', q, kg, preferred_element_type=jnp.float32)
    s = jnp.where(valid[None, :, None, :], s, -jnp.inf)
    m = jnp.max(s, axis=-1)
    p = jnp.exp(s - m[..., None])
    l = jnp.sum(p, axis=-1)
    acc = jnp.einsum('bthj,btjhd->bthd', p, vg.astype(jnp.float32))
    return m, l, acc


def combine_branches(parts):
    m = jnp.stack([p_[0] for p_ in parts])
    l = jnp.stack([p_[1] for p_ in parts])
    acc = jnp.stack([p_[2] for p_ in parts])
    w = jnp.exp(m - jnp.max(m, axis=0, keepdims=True))
    den = jnp.sum(w * l, axis=0)
    num = jnp.sum(w[..., None] * acc, axis=0)
    return num / den[..., None]


def attention_prompt(q, k, v):
    parts = [dilated_branch_prompt(q, k, v, d, w // d) for (w, d) in ATT_BRANCHES]
    out = combine_branches(parts).astype(q.dtype)
    keep = min(ATT_WINDOW, q.shape[1])
    return out, k[:, -keep:], v[:, -keep:]


def attention_sample(q, k, v, cache_k, cache_v):
    n_buf = cache_k.shape[1]
    kk = jnp.concatenate([cache_k.astype(k.dtype), k], axis=1)
    vv = jnp.concatenate([cache_v.astype(v.dtype), v], axis=1)
    parts = [dilated_branch_sample(q, kk, vv, d, w // d, n_buf) for (w, d) in ATT_BRANCHES]
    out = combine_branches(parts).astype(q.dtype)
    return out, kk[:, -n_buf:], vv[:, -n_buf:]


def hgrn2_scan(q, k, i, logf, S0):
    B, L, H, Dk = q.shape
    Dv = i.shape[-1]
    C = HGRN_CHUNK if L % HGRN_CHUNK == 0 else L
    n = L // C

    def chunks(t):
        return jnp.swapaxes(t.reshape(B, n, C, H, t.shape[-1]), 0, 1)

    causal = jnp.arange(C)[:, None] >= jnp.arange(C)[None, :]

    def step(S, inp):
        qc, kc, ic, gc = inp
        b = jnp.cumsum(gc, axis=1)
        inter = jnp.einsum('bchk,bhkv->bchv', qc * jnp.exp(b), S)
        diff = b[:, :, None] - b[:, None, :]
        decay = jnp.exp(jnp.where(causal[None, :, :, None, None], diff, -jnp.inf))
        A = jnp.einsum('bthk,btshk,bshk->bhts', qc, decay, kc)
        intra = jnp.einsum('bhts,bshv->bthv', A, ic)
        b_last = b[:, -1]
        S_new = jnp.exp(b_last)[..., None] * S + jnp.einsum(
            'bshk,bshv->bhkv', kc * jnp.exp(b_last[:, None] - b), ic)
        return S_new, inter + intra

    S_fin, o = lax.scan(step, S0, (chunks(q), chunks(k), chunks(i), chunks(logf)))
    o = jnp.swapaxes(o, 0, 1).reshape(B, L, H, Dv)
    return o, S_fin


def peer(xn, wq, sub_keys, u, v):
    B, T, D = xn.shape
    n = B * T
    xt = xn.reshape(n, D)
    half = PEER_DKEY // 2
    q = (xt @ wq).reshape(n, PEER_HEADS, PEER_DKEY).astype(jnp.float32)
    sk = sub_keys.astype(jnp.float32)
    s1 = jnp.einsum('nhd,hkd->nhk', q[..., :half], sk[:, 0])
    s2 = jnp.einsum('nhd,hkd->nhk', q[..., half:], sk[:, 1])
    v1, i1 = lax.top_k(s1, PEER_TOPK)
    v2, i2 = lax.top_k(s2, PEER_TOPK)
    cand = (v1[..., :, None] + v2[..., None, :]).reshape(n, PEER_HEADS, PEER_TOPK * PEER_TOPK)
    cidx = (i1[..., :, None] * N_KEYS + i2[..., None, :]).reshape(n, PEER_HEADS, PEER_TOPK * PEER_TOPK)
    sv, si = lax.top_k(cand, PEER_TOPK)
    e = jnp.take_along_axis(cidx, si, axis=-1)
    g = jax.nn.softmax(sv, axis=-1)
    pad = (-n) % PEER_BLOCK
    nb = (n + pad) // PEER_BLOCK
    xp = jnp.pad(xt, ((0, pad), (0, 0))).reshape(nb, PEER_BLOCK, D)
    ep = jnp.pad(e, ((0, pad), (0, 0), (0, 0))).reshape(nb, PEER_BLOCK, PEER_HEADS, PEER_TOPK)
    gp = jnp.pad(g, ((0, pad), (0, 0), (0, 0))).reshape(nb, PEER_BLOCK, PEER_HEADS, PEER_TOPK)

    def block(args):
        xb, eb, gb = args
        ue = u[eb]
        act = jax.nn.gelu(jnp.einsum('nd,nhkd->nhk', xb, ue, preferred_element_type=jnp.float32),
                          approximate=False)
        return jnp.einsum('nhk,nhkd->nd', (gb * act).astype(v.dtype), v[eb],
                          preferred_element_type=jnp.float32)

    y = lax.map(block, (xp, ep, gp)).reshape(nb * PEER_BLOCK, D)[:n]
    return y.reshape(B, T, D).astype(xn.dtype)


def layer(x, attn_fn, S0, lb, norm1_w, w_in, hgrn_norm_w, w_out, norm2_w,
          peer_wq, peer_sub_keys, peer_u, peer_v):
    B, T, _ = x.shape
    xn = rmsnorm(x, norm1_w)
    proj = xn @ w_in
    aq, ak, av, hq, hf, hi, hg = jnp.split(proj, list(np.cumsum(IN_SPLITS)[:-1]), axis=-1)
    att, k_state, v_state = attn_fn(
        aq.reshape(B, T, ATT_HEADS, HEAD_DIM) * ATT_SCALE,
        ak.reshape(B, T, ATT_HEADS, HEAD_DIM),
        av.reshape(B, T, ATT_HEADS, HEAD_DIM))
    f = lb + (1.0 - lb) * jax.nn.sigmoid(hf.astype(jnp.float32))
    hh = lambda t, d: t.reshape(B, T, HGRN_HEADS, d)
    o, S = hgrn2_scan(hh(jax.nn.silu(hq.astype(jnp.float32)), HGRN_DK),
                      hh(1.0 - f, HGRN_DK),
                      hh(hi.astype(jnp.float32), HGRN_DV),
                      hh(jnp.log(f), HGRN_DK),
                      S0)
    o = o * lax.rsqrt(jnp.mean(o * o, axis=-1, keepdims=True) + EPS) * hgrn_norm_w.astype(jnp.float32)
    o = o * jax.nn.silu(hh(hg.astype(jnp.float32), HGRN_DV))
    mix = jnp.concatenate([att.reshape(B, T, ATT_WIDTH),
                           o.reshape(B, T, HGRN_WIDTH).astype(x.dtype)], axis=-1)
    h = x + mix @ w_out
    h = h + peer(rmsnorm(h, norm2_w), peer_wq, peer_sub_keys, peer_u, peer_v)
    return h, k_state, v_state, S


def setup_inputs(seed: int = 0) -> dict:
    key = jax.random.key(seed)
    ks = jax.random.split(key, 20)
    f32 = jnp.float32
    win_buf = min(ATT_WINDOW, PAST_LEN)
    nrm = lambda k, shape, s: jax.random.normal(k, shape, f32) * s
    return {
        "x_prompt": nrm(ks[0], (BATCH, SEQ, D_MODEL), 1.0),
        "x_sample": nrm(ks[1], (DEC_BATCH, DEC_SEQ, D_MODEL), 1.0),
        "cache_k_win": nrm(ks[2], (DEPTH, DEC_BATCH, win_buf, ATT_HEADS, HEAD_DIM), 1.0),
        "cache_v_win": nrm(ks[3], (DEPTH, DEC_BATCH, win_buf, ATT_HEADS, HEAD_DIM), 1.0),
        "state_hgrn": nrm(ks[4], (DEPTH, DEC_BATCH, HGRN_HEADS, HGRN_DK, HGRN_DV), 0.5),
        "norm1_w": 1.0 + nrm(ks[5], (DEPTH, D_MODEL), 0.02),
        "w_in": nrm(ks[6], (DEPTH, D_MODEL, IN_COLS), D_MODEL ** -0.5),
        "hgrn_norm_w": 1.0 + nrm(ks[7], (DEPTH, HGRN_DV), 0.02),
        "hgrn_lb_logits": nrm(ks[8], (DEPTH + 1, HGRN_WIDTH), 0.1),
        "w_out": nrm(ks[9], (DEPTH, MIX_WIDTH, D_MODEL), MIX_WIDTH ** -0.5),
        "norm2_w": 1.0 + nrm(ks[10], (DEPTH, D_MODEL), 0.02),
        "peer_wq": nrm(ks[11], (DEPTH, D_MODEL, PEER_HEADS * PEER_DKEY), D_MODEL ** -0.5),
        "peer_sub_keys": nrm(ks[12], (DEPTH, PEER_HEADS, 2, N_KEYS, PEER_DKEY // 2), (PEER_DKEY // 2) ** -0.5),
        "peer_u": nrm(ks[13], (DEPTH, N_EXPERTS, D_MODEL), D_MODEL ** -0.5),
        "peer_v": nrm(ks[14], (DEPTH, N_EXPERTS, D_MODEL), 0.5),
        "norm_f_w": 1.0 + nrm(ks[15], (D_MODEL,), 0.02),
    }


def reference(x_prompt, x_sample, cache_k_win, cache_v_win, state_hgrn, norm1_w, w_in,
              hgrn_norm_w, hgrn_lb_logits, w_out, norm2_w, peer_wq, peer_sub_keys,
              peer_u, peer_v, norm_f_w):
    lb_all = jnp.cumsum(jax.nn.softmax(hgrn_lb_logits.astype(jnp.float32), axis=0), axis=0)
    hp, hs = x_prompt, x_sample
    kp, vp, sp, ksm, vsm, ssm = [], [], [], [], [], []
    for l in range(DEPTH):
        w = (norm1_w[l], w_in[l], hgrn_norm_w[l], w_out[l], norm2_w[l],
             peer_wq[l], peer_sub_keys[l], peer_u[l], peer_v[l])
        S0p = jnp.zeros((hp.shape[0], HGRN_HEADS, HGRN_DK, HGRN_DV), jnp.float32)
        hp, k_, v_, s_ = layer(hp, attention_prompt, S0p, lb_all[l], *w)
        kp.append(k_); vp.append(v_); sp.append(s_.astype(x_prompt.dtype))
        attn_s = functools.partial(attention_sample, cache_k=cache_k_win[l], cache_v=cache_v_win[l])
        hs, k_, v_, s_ = layer(hs, attn_s, state_hgrn[l].astype(jnp.float32), lb_all[l], *w)
        ksm.append(k_); vsm.append(v_); ssm.append(s_.astype(state_hgrn.dtype))
    y_prompt = rmsnorm(hp, norm_f_w)
    y_sample = rmsnorm(hs, norm_f_w)
    return (y_prompt, y_sample, jnp.stack(kp), jnp.stack(vp), jnp.stack(sp),
            jnp.stack(ksm), jnp.stack(vsm), jnp.stack(ssm))
```

```python
import functools
import math

import jax
import jax.numpy as jnp
from jax import lax
from jax.experimental import pallas as pl
from jax.experimental.pallas import tpu as pltpu

F32 = jnp.float32
BF16 = jnp.bfloat16

D_MODEL = 1024
HEAD_DIM = 64
ATT_WIDTH = 512
ATT_HEADS = 8
ATT_BRANCHES = ((128, 1), (512, 4), (2048, 16))
ATT_STEPS = 128
ATT_SCALE = HEAD_DIM ** -0.5
HGRN_WIDTH = 512
HGRN_DK = 128
HGRN_HEADS = 4
HGRN_CHUNK = 16
IN_COLS = 3 * ATT_WIDTH + 4 * HGRN_WIDTH
N_KEYS = 128
PEER_HEADS = 8
PEER_DKEY = 256
PEER_TOPK = 16
EPS = 1e-6
NEG_INF = float("-inf")

LANES = 128
SUBLANES = 8
PEER_EXPERT_BLOCK = SUBLANES * N_KEYS
VMEM_LIMIT = 56 * 1024 * 1024


def _dot(a, b):
    return jnp.dot(a, b, preferred_element_type=F32)


def _dot_nt(a, b):
    return lax.dot_general(a, b, (((1,), (1,)), ((), ())), preferred_element_type=F32)


def _dot_tn(a, b):
    return lax.dot_general(a, b, (((0,), (0,)), ((), ())), preferred_element_type=F32)


def _rms(x, w):
    return x * lax.rsqrt(jnp.mean(x * x, axis=-1, keepdims=True) + EPS) * w


def _params(*sem):
    return pltpu.CompilerParams(dimension_semantics=sem, vmem_limit_bytes=VMEM_LIMIT)


def _in_proj_kernel(x_ref, nw_ref, w_ref, q_ref, k_ref, v_ref, g_ref):
    xn = _rms(x_ref[...], nw_ref[...]).astype(BF16)
    p = _dot(xn, w_ref[...])
    q_ref[...] = p[:, :ATT_WIDTH] * ATT_SCALE
    k_ref[...] = p[:, ATT_WIDTH:2 * ATT_WIDTH]
    v_ref[...] = p[:, 2 * ATT_WIDTH:3 * ATT_WIDTH]
    g_ref[...] = p[:, 3 * ATT_WIDTH:]


def _in_proj(x, norm_w, w_in_b, tm):
    n = x.shape[0]
    row = lambda i: (i, 0)
    fixed = lambda i: (0, 0)
    return pl.pallas_call(
        _in_proj_kernel,
        grid=(n // tm,),
        in_specs=[pl.BlockSpec((tm, D_MODEL), row),
                  pl.BlockSpec((1, D_MODEL), fixed),
                  pl.BlockSpec((D_MODEL, IN_COLS), fixed)],
        out_specs=[pl.BlockSpec((tm, ATT_WIDTH), row),
                   pl.BlockSpec((tm, ATT_WIDTH), row),
                   pl.BlockSpec((tm, ATT_WIDTH), row),
                   pl.BlockSpec((tm, 4 * HGRN_WIDTH), row)],
        out_shape=[jax.ShapeDtypeStruct((n, ATT_WIDTH), F32),
                   jax.ShapeDtypeStruct((n, ATT_WIDTH), F32),
                   jax.ShapeDtypeStruct((n, ATT_WIDTH), F32),
                   jax.ShapeDtypeStruct((n, 4 * HGRN_WIDTH), F32)],
        compiler_params=_params("parallel"),
        name="in_proj",
    )(x, norm_w, w_in_b)


def _attn_prompt_kernel(q_ref, k_ref, v_ref, o_ref, m_scr, l_scr, a_scr, *, seq):
    blk = ATT_STEPS
    row = lax.broadcasted_iota(jnp.int32, (blk, blk), 0)
    col = lax.broadcasted_iota(jnp.int32, (blk, blk), 1)
    own_ok = col <= row
    prev_ok = col >= row

    def rows_of(start, dil):
        return pl.ds(start, blk) if dil == 1 else pl.ds(start, blk, stride=dil)

    def do_block(br, dil, start, prev_start, prev_bias):
        rows = rows_of(start, dil)
        qb = q_ref[rows, :].astype(BF16)
        kb = k_ref[rows, :].astype(BF16)
        vb = v_ref[rows, :].astype(BF16)
        if prev_start is not None:
            prows = rows_of(prev_start, dil)
            kp = k_ref[prows, :].astype(BF16)
            vp = v_ref[prows, :].astype(BF16)
        ms, ls, accs = [], [], []
        for hh in range(2):
            sl = slice(hh * HEAD_DIM, (hh + 1) * HEAD_DIM)
            s_own = jnp.where(own_ok, _dot_nt(qb[:, sl], kb[:, sl]), NEG_INF)
            m = jnp.max(s_own, axis=-1, keepdims=True)
            if prev_start is not None:
                s_prev = jnp.where(prev_ok, _dot_nt(qb[:, sl], kp[:, sl]) + prev_bias, NEG_INF)
                m = jnp.maximum(m, jnp.max(s_prev, axis=-1, keepdims=True))
            p_own = jnp.exp(s_own - m)
            l = jnp.sum(p_own, axis=-1, keepdims=True)
            acc = _dot(p_own.astype(BF16), vb[:, sl])
            if prev_start is not None:
                p_prev = jnp.exp(s_prev - m)
                l = l + jnp.sum(p_prev, axis=-1, keepdims=True)
                acc = acc + _dot(p_prev.astype(BF16), vp[:, sl])
            ms.append(jnp.broadcast_to(m, (blk, HEAD_DIM)))
            ls.append(jnp.broadcast_to(l, (blk, HEAD_DIM)))
            accs.append(acc)
        m_scr[br, rows, :] = jnp.concatenate(ms, axis=1)
        l_scr[br, rows, :] = jnp.concatenate(ls, axis=1)
        a_scr[br, rows, :] = jnp.concatenate(accs, axis=1)

    for br, (window, dil) in enumerate(ATT_BRANCHES):
        class_len = seq // dil
        nb = class_len // blk

        if nb == 1:
            def body(r, carry, br=br, dil=dil):
                do_block(br, dil, r, None, None)
                return carry
            lax.fori_loop(0, dil, body, 0)
        else:
            def body(idx, carry, br=br, dil=dil):
                n = idx // dil
                r = idx - n * dil
                start = n * (blk * dil) + r
                prev_start = jnp.maximum(n - 1, 0) * (blk * dil) + r
                prev_bias = jnp.where(n > 0, 0.0, NEG_INF).astype(F32)
                do_block(br, dil, start, prev_start, prev_bias)
                return carry
            lax.fori_loop(0, nb * dil, body, 0)

    chunk = 256

    def combine(c, carry):
        rows = pl.ds(pl.multiple_of(c * chunk, chunk), chunk)
        m = [m_scr[b, rows, :] for b in range(3)]
        mx = jnp.maximum(jnp.maximum(m[0], m[1]), m[2])
        den = jnp.zeros((chunk, LANES), F32)
        num = jnp.zeros((chunk, LANES), F32)
        for b in range(3):
            w = jnp.exp(m[b] - mx)
            den = den + w * l_scr[b, rows, :]
            num = num + w * a_scr[b, rows, :]
        o_ref[rows, :] = num / den
        return carry

    lax.fori_loop(0, seq // chunk, combine, 0)


def _attn_prompt(q, k, v, batch, seq):
    spec = pl.BlockSpec((seq, LANES), lambda b, hp: (b, hp))
    return pl.pallas_call(
        functools.partial(_attn_prompt_kernel, seq=seq),
        grid=(batch, ATT_WIDTH // LANES),
        in_specs=[spec, spec, spec],
        out_specs=spec,
        out_shape=jax.ShapeDtypeStruct((batch * seq, ATT_WIDTH), F32),
        scratch_shapes=[pltpu.VMEM((3, seq, LANES), F32)] * 3,
        compiler_params=_params("parallel", "parallel"),
        name="attn_prompt",
    )(q, k, v)


def _attn_sample_kernel(q_ref, kn_ref, vn_ref, kc_ref, vc_ref, o_ref, *, n_new, n_buf, n_pad):
    nrow = n_new * ATT_HEADS
    lane = lax.broadcasted_iota(jnp.int32, (nrow, ATT_WIDTH), 1)
    rown = lax.broadcasted_iota(jnp.int32, (nrow, ATT_WIDTH), 0)
    head_ok = (lane // HEAD_DIM) == (rown % ATT_HEADS)
    q = jnp.where(head_ok, q_ref[0], 0.0).astype(BF16)
    kc = kc_ref[0].astype(BF16)
    vc = vc_ref[0].astype(BF16)
    kn = kn_ref[0].astype(BF16)
    vn = vn_ref[0].astype(BF16)
    s_c = _dot_nt(q, kc)
    s_n = _dot_nt(q, kn)
    tok_c = lax.broadcasted_iota(jnp.int32, (nrow, n_buf), 0) // ATT_HEADS
    dist_c = n_buf + tok_c - lax.broadcasted_iota(jnp.int32, (nrow, n_buf), 1)
    tok_n = lax.broadcasted_iota(jnp.int32, (nrow, n_pad), 0) // ATT_HEADS
    key_n = lax.broadcasted_iota(jnp.int32, (nrow, n_pad), 1)
    dist_n = tok_n - key_n
    parts = []
    for window, dil in ATT_BRANCHES:
        ok_c = ((dist_c & (dil - 1)) == 0) & (dist_c <= window)
        ok_n = ((dist_n & (dil - 1)) == 0) & (dist_n >= 0) & (key_n < n_new)
        sc = jnp.where(ok_c, s_c, NEG_INF)
        sn = jnp.where(ok_n, s_n, NEG_INF)
        m = jnp.maximum(jnp.max(sc, axis=-1, keepdims=True), jnp.max(sn, axis=-1, keepdims=True))
        pc = jnp.exp(sc - m)
        pn = jnp.exp(sn - m)
        l = jnp.sum(pc, axis=-1, keepdims=True) + jnp.sum(pn, axis=-1, keepdims=True)
        acc = _dot(pc.astype(BF16), vc) + _dot(pn.astype(BF16), vn)
        parts.append((m, l, acc))
    mx = jnp.maximum(jnp.maximum(parts[0][0], parts[1][0]), parts[2][0])
    den = jnp.zeros((nrow, 1), F32)
    num = jnp.zeros((nrow, ATT_WIDTH), F32)
    for m, l, acc in parts:
        w = jnp.exp(m - mx)
        den = den + w * l
        num = num + w * acc
    out = jnp.where(head_ok, num / den, 0.0)
    o_ref[0] = jnp.sum(out.reshape(n_new, ATT_HEADS, ATT_WIDTH), axis=1)


def _attn_sample(q_rep, k_new, v_new, cache_k, cache_v, n_new):
    batch, n_buf, _ = cache_k.shape
    n_pad = k_new.shape[1]
    nrow = n_new * ATT_HEADS
    b3 = lambda b: (b, 0, 0)
    return pl.pallas_call(
        functools.partial(_attn_sample_kernel, n_new=n_new, n_buf=n_buf, n_pad=n_pad),
        grid=(batch,),
        in_specs=[pl.BlockSpec((1, nrow, ATT_WIDTH), b3),
                  pl.BlockSpec((1, n_pad, ATT_WIDTH), b3),
                  pl.BlockSpec((1, n_pad, ATT_WIDTH), b3),
                  pl.BlockSpec((1, n_buf, ATT_WIDTH), b3),
                  pl.BlockSpec((1, n_buf, ATT_WIDTH), b3)],
        out_specs=pl.BlockSpec((1, n_new, ATT_WIDTH), b3),
        out_shape=jax.ShapeDtypeStruct((batch, n_new, ATT_WIDTH), F32),
        compiler_params=_params("parallel"),
        name="attn_sample",
    )(q_rep, k_new, v_new, cache_k, cache_v)


def _hgrn_kernel(*refs, seq, n_valid, has_state, layer):
    if has_state:
        hq_ref, hf_ref, hi_ref, hg_ref, lbl_ref, nw_ref, s0_ref, o_ref, sfin_ref, st_scr = refs
    else:
        hq_ref, hf_ref, hi_ref, hg_ref, lbl_ref, nw_ref, o_ref, sfin_ref, st_scr = refs
    C = HGRN_CHUNK
    if has_state:
        st_scr[...] = s0_ref[0, 0].T
    else:
        st_scr[...] = jnp.zeros((HGRN_DK, HGRN_DK), F32)
    lg = lbl_ref[...]
    ex = jnp.exp(lg - jnp.max(lg, axis=0, keepdims=True))
    lb = jnp.sum(ex[:layer + 1], axis=0, keepdims=True) / jnp.sum(ex, axis=0, keepdims=True)
    nw = nw_ref[...]
    rowi = lax.broadcasted_iota(jnp.int32, (C, HGRN_DK), 0)
    ones = jnp.ones((HGRN_DK, LANES), BF16)

    def body(c, carry):
        r0 = pl.multiple_of(c * C, C)
        rows = pl.ds(r0, C)
        f = lb + (1.0 - lb) * jax.nn.sigmoid(hf_ref[rows, :])
        logf = jnp.log(f)
        kk = 1.0 - f
        if n_valid < seq:
            live = (rowi + r0) < n_valid
            logf = jnp.where(live, logf, 0.0)
            kk = jnp.where(live, kk, 0.0)
        q = jax.nn.silu(hq_ref[rows, :])
        ii = hi_ref[rows, :]
        b = logf
        sh = 1
        while sh < C:
            b = b + jnp.where(rowi >= sh, pltpu.roll(b, sh, axis=0), 0.0)
            sh *= 2
        st = st_scr[...]
        inter = _dot_nt((q * jnp.exp(b)).astype(BF16), st.astype(BF16))
        prods = []
        for t in range(C):
            e = jnp.exp(jnp.where(rowi <= t, b[t:t + 1, :] - b, NEG_INF))
            prods.append(q[t:t + 1, :] * e * kk)
        a_rep = _dot(jnp.concatenate(prods, axis=0).astype(BF16), ones)
        intra = jnp.concatenate(
            [jnp.sum(a_rep[t * C:(t + 1) * C, :] * ii, axis=0, keepdims=True) for t in range(C)],
            axis=0)
        o = inter + intra
        b_last = b[C - 1:C, :]
        kdec = kk * jnp.exp(b_last - b)
        st_scr[...] = st * jnp.exp(b_last) + _dot_tn(ii.astype(BF16), kdec.astype(BF16))
        g = hg_ref[rows, :]
        o_ref[rows, :] = _rms(o, nw) * jax.nn.silu(g)
        return carry

    lax.fori_loop(0, seq // C, body, 0)
    sfin_ref[0, 0] = st_scr[...].T


def _hgrn(g4, lb_logits, norm_w, state, batch, seq, n_valid, layer):
    has_state = state is not None
    nh = HGRN_HEADS
    col = lambda part: pl.BlockSpec((seq, HGRN_DK), lambda b, h, part=part: (b, part * nh + h))
    st_spec = pl.BlockSpec((1, 1, HGRN_DK, HGRN_DK), lambda b, h: (b, h, 0, 0))
    in_specs = [col(0), col(1), col(2), col(3),
                pl.BlockSpec((lb_logits.shape[0], HGRN_DK), lambda b, h: (0, h)),
                pl.BlockSpec((1, HGRN_DK), lambda b, h: (0, 0))]
    args = [g4, g4, g4, g4, lb_logits, norm_w]
    if has_state:
        in_specs.append(st_spec)
        args.append(state)
    return pl.pallas_call(
        functools.partial(_hgrn_kernel, seq=seq, n_valid=n_valid, has_state=has_state, layer=layer),
        grid=(batch, nh),
        in_specs=in_specs,
        out_specs=[pl.BlockSpec((seq, HGRN_DK), lambda b, h: (b, h)), st_spec],
        out_shape=[jax.ShapeDtypeStruct((batch * seq, HGRN_WIDTH), F32),
                   jax.ShapeDtypeStruct((batch, nh, HGRN_DK, HGRN_DK), F32)],
        scratch_shapes=[pltpu.VMEM((HGRN_DK, HGRN_DK), F32)],
        compiler_params=_params("parallel", "parallel"),
        name="hgrn",
    )(*args)


def _out_proj_kernel(x_ref, att_ref, og_ref, wo_ref, nw_ref, wq_ref, h_ref, xn_ref, qp_ref):
    h = (x_ref[...]
         + _dot(att_ref[...].astype(BF16), wo_ref[:ATT_WIDTH, :])
         + _dot(og_ref[...].astype(BF16), wo_ref[ATT_WIDTH:, :]))
    h_ref[...] = h
    xn = _rms(h, nw_ref[...]).astype(BF16)
    xn_ref[...] = xn
    qp_ref[...] = _dot(xn, wq_ref[...])


def _out_proj(x, att, og, w_out_b, norm_w, wq_b, tm):
    n = x.shape[0]
    qcols = wq_b.shape[1]
    row = lambda i: (i, 0)
    fixed = lambda i: (0, 0)
    return pl.pallas_call(
        _out_proj_kernel,
        grid=(n // tm,),
        in_specs=[pl.BlockSpec((tm, D_MODEL), row),
                  pl.BlockSpec((tm, ATT_WIDTH), row),
                  pl.BlockSpec((tm, HGRN_WIDTH), row),
                  pl.BlockSpec((ATT_WIDTH + HGRN_WIDTH, D_MODEL), fixed),
                  pl.BlockSpec((1, D_MODEL), fixed),
                  pl.BlockSpec((D_MODEL, qcols), fixed)],
        out_specs=[pl.BlockSpec((tm, D_MODEL), row),
                   pl.BlockSpec((tm, D_MODEL), row),
                   pl.BlockSpec((tm, qcols), row)],
        out_shape=[jax.ShapeDtypeStruct((n, D_MODEL), F32),
                   jax.ShapeDtypeStruct((n, D_MODEL), BF16),
                   jax.ShapeDtypeStruct((n, qcols), F32)],
        compiler_params=_params("parallel"),
        name="out_proj",
    )(x, att, og, w_out_b, norm_w, wq_b)


def _top_values(s, count):
    rows = lax.broadcasted_iota(jnp.int32, (count, s.shape[1]), 0)
    out = jnp.zeros((count, s.shape[1]), F32)
    cur = s
    for r in range(count):
        m = jnp.max(cur, axis=0, keepdims=True)
        out = jnp.where(rows == r, m, out)
        if r + 1 < count:
            cur = jnp.where(cur == m, NEG_INF, cur)
    return out


def _peer_scores_kernel(qp_ref, keys_ref, s2_ref, th_ref, e1_ref, e2_ref):
    tn = qp_ref.shape[0]
    k = PEER_TOPK
    half = PEER_DKEY // 2
    rowk = lax.broadcasted_iota(jnp.int32, (k, tn), 0)
    for h in range(PEER_HEADS):
        q1 = qp_ref[:, h * PEER_DKEY:h * PEER_DKEY + half].astype(BF16)
        q2 = qp_ref[:, h * PEER_DKEY + half:(h + 1) * PEER_DKEY].astype(BF16)
        s1 = _dot_nt(keys_ref[h, 0], q1)
        s2 = _dot_nt(keys_ref[h, 1], q2)
        t1 = _top_values(s1, k)
        t2 = _top_values(s2, k)
        cands = []
        for i in range(k // 2):
            cands.append(jnp.where(rowk < k // (i + 1), t1[i:i + 1, :] + t2, NEG_INF))
        cands.append(t1[k // 2:, :] + t2[0:1, :])
        cand = jnp.concatenate(cands, axis=0)
        top = t1[0:1, :] + t2[0:1, :]
        cur = cand
        for _ in range(k - 1):
            cur = jnp.where(cur == jnp.max(cur, axis=0, keepdims=True), NEG_INF, cur)
        tau = jnp.max(cur, axis=0, keepdims=True)
        z = jnp.sum(jnp.where(cand >= tau, jnp.exp(cand - top), 0.0), axis=0, keepdims=True)
        s2_ref[h] = s2
        th_ref[h] = tau - s1
        e1_ref[h] = jnp.exp(s1 - t1[0:1, :])
        e2_ref[h] = jnp.exp(s2 - t2[0:1, :]) / z


def _peer_scores(qp, keys_b, tn):
    n = qp.shape[0]
    out_spec = pl.BlockSpec((PEER_HEADS, N_KEYS, tn), lambda i: (0, 0, i))
    out_shape = jax.ShapeDtypeStruct((PEER_HEADS, N_KEYS, n), F32)
    return pl.pallas_call(
        _peer_scores_kernel,
        grid=(n // tn,),
        in_specs=[pl.BlockSpec((tn, PEER_HEADS * PEER_DKEY), lambda i: (i, 0)),
                  pl.BlockSpec((PEER_HEADS, 2, N_KEYS, PEER_DKEY // 2), lambda i: (0, 0, 0, 0))],
        out_specs=[out_spec] * 4,
        out_shape=[out_shape] * 4,
        compiler_params=_params("parallel"),
        name="peer_scores",
    )(qp, keys_b)


def _peer_kernel(x_ref, h_ref, u_ref, vt_ref, s2_ref, th_ref, e1_ref, e2_ref, nfw_ref,
                 o_ref, acc_scr, z_scr, c_scr, *, eb, tn):
    j = pl.program_id(1)

    @pl.when(j == 0)
    def _():
        acc_scr[...] = jnp.zeros_like(acc_scr)

    z_scr[...] = _dot_nt(u_ref[...], x_ref[...])
    slabs = eb // N_KEYS
    assert slabs == SUBLANES
    arows = pl.ds(pl.multiple_of(j * slabs, slabs), slabs)
    for al in range(slabs):
        for tc in range(tn // LANES):
            lanes = slice(tc * LANES, (tc + 1) * LANES)
            w = jnp.zeros((N_KEYS, LANES), F32)
            for h in range(PEER_HEADS):
                th = th_ref[h, arows, lanes][al:al + 1, :]
                e1 = e1_ref[h, arows, lanes][al:al + 1, :]
                w = w + jnp.where(s2_ref[h, :, lanes] >= th, e2_ref[h, :, lanes], 0.0) * e1
            z = z_scr[al * N_KEYS:(al + 1) * N_KEYS, lanes]
            act = 0.5 * z * (1.0 + lax.erf(z * math.sqrt(0.5)))
            c_scr[al * N_KEYS:(al + 1) * N_KEYS, lanes] = (w * act).astype(BF16)
    acc_scr[...] += _dot(vt_ref[...], c_scr[...])

    @pl.when(j == pl.num_programs(1) - 1)
    def _():
        hh = h_ref[...] + acc_scr[...].T
        o_ref[...] = _rms(hh, nfw_ref[...])


def _peer(xn, h, u_b, vt_b, s2, th, e1, e2, nf_w, tn, eb):
    n = xn.shape[0]
    n_exp = u_b.shape[0]
    tok = pl.BlockSpec((tn, D_MODEL), lambda i, j: (i, 0))
    sc = pl.BlockSpec((PEER_HEADS, N_KEYS, tn), lambda i, j: (0, 0, i))
    return pl.pallas_call(
        functools.partial(_peer_kernel, eb=eb, tn=tn),
        grid=(n // tn, n_exp // eb),
        in_specs=[tok, tok,
                  pl.BlockSpec((eb, D_MODEL), lambda i, j: (j, 0)),
                  pl.BlockSpec((D_MODEL, eb), lambda i, j: (0, j)),
                  sc, sc, sc, sc,
                  pl.BlockSpec((1, D_MODEL), lambda i, j: (0, 0))],
        out_specs=tok,
        out_shape=jax.ShapeDtypeStruct((n, D_MODEL), F32),
        scratch_shapes=[pltpu.VMEM((D_MODEL, tn), F32),
                        pltpu.VMEM((eb, tn), F32),
                        pltpu.VMEM((eb, tn), BF16)],
        compiler_params=_params("parallel", "arbitrary"),
        name="peer_experts",
    )(xn, h, u_b, vt_b, s2, th, e1, e2, nf_w)


def _tail(x, att, og, w, tm, tn, eb):
    h, xn, qp = _out_proj(x, att, og, w["w_out"], w["norm2"], w["wq"], tm)
    s2, th, e1, e2 = _peer_scores(qp, w["keys"], min(tn, 256))
    return _peer(xn, h, w["u"], w["vt"], s2, th, e1, e2, w["norm_f"], tn, eb)


def kernel(x_prompt, x_sample, cache_k_win, cache_v_win, state_hgrn, norm1_w, w_in,
           hgrn_norm_w, hgrn_lb_logits, w_out, norm2_w, peer_wq, peer_sub_keys,
           peer_u, peer_v, norm_f_w):
    layer = 0
    batch, seq, _ = x_prompt.shape
    dec_batch, dec_seq, _ = x_sample.shape
    w = {
        "norm1": norm1_w[layer][None, :],
        "w_in": w_in[layer].astype(BF16),
        "hgrn_norm": hgrn_norm_w[layer][None, :],
        "w_out": w_out[layer].astype(BF16),
        "norm2": norm2_w[layer][None, :],
        "wq": peer_wq[layer].astype(BF16),
        "keys": peer_sub_keys[layer].astype(BF16),
        "u": peer_u[layer].astype(BF16),
        "vt": peer_v[layer].astype(BF16).T,
        "norm_f": norm_f_w[None, :],
    }

    xp = x_prompt.reshape(batch * seq, D_MODEL)
    q, k, v, g4 = _in_proj(xp, w["norm1"], w["w_in"], 256)
    att = _attn_prompt(q, k, v, batch, seq)
    og, s_prompt = _hgrn(g4, hgrn_lb_logits, w["hgrn_norm"], None, batch, seq, seq, layer)
    y_prompt = _tail(xp, att, og, w, 256, 512, PEER_EXPERT_BLOCK).reshape(batch, seq, D_MODEL)

    xs = x_sample.reshape(dec_batch * dec_seq, D_MODEL)
    tms = dec_batch * dec_seq
    qs, ks, vs, g4s = _in_proj(xs, w["norm1"], w["w_in"], tms)
    n_pad = LANES
    q_rep = jnp.repeat(qs.reshape(dec_batch, dec_seq, ATT_WIDTH), ATT_HEADS, axis=1)
    pad_new = lambda t: jnp.pad(t.reshape(dec_batch, dec_seq, ATT_WIDTH),
                                ((0, 0), (0, n_pad - dec_seq), (0, 0)))
    n_buf = cache_k_win.shape[2]
    ck = cache_k_win[layer].reshape(dec_batch, n_buf, ATT_WIDTH)
    cv = cache_v_win[layer].reshape(dec_batch, n_buf, ATT_WIDTH)
    att_s = _attn_sample(q_rep, pad_new(ks), pad_new(vs), ck, cv, dec_seq)
    att_s = att_s.reshape(tms, ATT_WIDTH)
    seq_pad = HGRN_CHUNK
    g4p = jnp.pad(g4s.reshape(dec_batch, dec_seq, 4 * HGRN_WIDTH),
                  ((0, 0), (0, seq_pad - dec_seq), (0, 0))).reshape(dec_batch * seq_pad, 4 * HGRN_WIDTH)
    ogp, s_sample = _hgrn(g4p, hgrn_lb_logits, w["hgrn_norm"], state_hgrn[layer],
                          dec_batch, seq_pad, dec_seq, layer)
    og_s = ogp.reshape(dec_batch, seq_pad, HGRN_WIDTH)[:, :dec_seq].reshape(tms, HGRN_WIDTH)
    y_sample = _tail(xs, att_s, og_s, w, tms, tms, PEER_EXPERT_BLOCK).reshape(dec_batch, dec_seq, D_MODEL)

    heads = (ATT_HEADS, HEAD_DIM)
    k_win_sample = jnp.concatenate(
        [cache_k_win[layer][:, dec_seq:], ks.reshape(dec_batch, dec_seq, *heads)], axis=1)
    v_win_sample = jnp.concatenate(
        [cache_v_win[layer][:, dec_seq:], vs.reshape(dec_batch, dec_seq, *heads)], axis=1)
    return (y_prompt, y_sample,
            k.reshape(1, batch, seq, *heads), v.reshape(1, batch, seq, *heads),
            s_prompt[None],
            k_win_sample[None], v_win_sample[None],
            s_sample.astype(state_hgrn.dtype)[None])
```

```python
import functools
import math

import jax
import jax.numpy as jnp
from jax import lax
from jax.experimental import pallas as pl
from jax.experimental.pallas import tpu as pltpu

F32 = jnp.float32
BF16 = jnp.bfloat16

D_MODEL = 1024
HEAD_DIM = 64
ATT_WIDTH = 512
ATT_HEADS = 8
ATT_BRANCHES = ((128, 1), (512, 4), (2048, 16))
ATT_STEPS = 128
ATT_SCALE = HEAD_DIM ** -0.5
HGRN_WIDTH = 512
HGRN_DK = 128
HGRN_HEADS = 4
HGRN_CHUNK = 16
IN_COLS = 3 * ATT_WIDTH + 4 * HGRN_WIDTH
N_KEYS = 128
PEER_HEADS = 8
PEER_DKEY = 256
PEER_TOPK = 16
EPS = 1e-6
NEG_INF = float("-inf")

LANES = 128
SUBLANES = 8
PEER_EXPERT_BLOCK = SUBLANES * N_KEYS
VMEM_LIMIT = 56 * 1024 * 1024


def _dot(a, b):
    return jnp.dot(a, b, preferred_element_type=F32)


def _dot_nt(a, b):
    return lax.dot_general(a, b, (((1,), (1,)), ((), ())), preferred_element_type=F32)


def _dot_tn(a, b):
    return lax.dot_general(a, b, (((0,), (0,)), ((), ())), preferred_element_type=F32)


def _rms(x, w):
    return x * lax.rsqrt(jnp.mean(x * x, axis=-1, keepdims=True) + EPS) * w


def _params(*sem):
    return pltpu.CompilerParams(dimension_semantics=sem, vmem_limit_bytes=VMEM_LIMIT)


def _in_proj_kernel(x_ref, nw_ref, w_ref, wkv_ref, q_ref, k_ref, v_ref, g_ref, kt_ref, vt_ref):
    xn = _rms(x_ref[...], nw_ref[...]).astype(BF16)
    p = _dot(xn, w_ref[...])
    q_ref[...] = p[:, :ATT_WIDTH] * ATT_SCALE
    k_ref[...] = p[:, ATT_WIDTH:2 * ATT_WIDTH]
    v_ref[...] = p[:, 2 * ATT_WIDTH:3 * ATT_WIDTH]
    g_ref[...] = p[:, 3 * ATT_WIDTH:]
    kvt = _dot_nt(wkv_ref[...], xn)
    kt_ref[0] = kvt[:ATT_WIDTH]
    vt_ref[0] = kvt[ATT_WIDTH:]


def _in_proj(x, norm_w, w_in_b, w_kvt_b, tm, seq):
    n = x.shape[0]
    per_seq = seq // tm
    row = lambda i: (i, 0)
    fixed = lambda i: (0, 0)
    tr = lambda i: (i // per_seq, 0, i % per_seq)
    return pl.pallas_call(
        _in_proj_kernel,
        grid=(n // tm,),
        in_specs=[pl.BlockSpec((tm, D_MODEL), row),
                  pl.BlockSpec((1, D_MODEL), fixed),
                  pl.BlockSpec((D_MODEL, IN_COLS), fixed),
                  pl.BlockSpec((2 * ATT_WIDTH, D_MODEL), fixed)],
        out_specs=[pl.BlockSpec((tm, ATT_WIDTH), row),
                   pl.BlockSpec((tm, ATT_WIDTH), row),
                   pl.BlockSpec((tm, ATT_WIDTH), row),
                   pl.BlockSpec((tm, 4 * HGRN_WIDTH), row),
                   pl.BlockSpec((1, ATT_WIDTH, tm), tr),
                   pl.BlockSpec((1, ATT_WIDTH, tm), tr)],
        out_shape=[jax.ShapeDtypeStruct((n, ATT_WIDTH), F32),
                   jax.ShapeDtypeStruct((n, ATT_WIDTH), F32),
                   jax.ShapeDtypeStruct((n, ATT_WIDTH), F32),
                   jax.ShapeDtypeStruct((n, 4 * HGRN_WIDTH), F32),
                   jax.ShapeDtypeStruct((n // seq, ATT_WIDTH, seq), F32),
                   jax.ShapeDtypeStruct((n // seq, ATT_WIDTH, seq), F32)],
        compiler_params=_params("parallel"),
        name="in_proj",
    )(x, norm_w, w_in_b, w_kvt_b)


def _attn_prompt_kernel(q_ref, k_ref, v_ref, o_ref, m_scr, l_scr, a_scr, *, seq):
    blk = ATT_STEPS
    lo = lax.broadcasted_iota(jnp.int32, (blk, LANES), 1) < HEAD_DIM

    def attend(br, dil, q_start, k_start, n_keys, off):
        stride = {} if dil == 1 else {"stride": dil}
        qrows = pl.ds(q_start, blk, **stride)
        krows = pl.ds(k_start, n_keys, **stride)
        qb = q_ref[qrows, :]
        kb = k_ref[krows, :].astype(BF16)
        vb = v_ref[krows, :]
        dist = (off + lax.broadcasted_iota(jnp.int32, (blk, n_keys), 0)
                - lax.broadcasted_iota(jnp.int32, (blk, n_keys), 1))
        ok = lax.bitcast_convert_type(dist, jnp.uint32) <= blk
        lo_k = lax.broadcasted_iota(jnp.int32, (n_keys, LANES), 1) < HEAD_DIM
        ms, rs = [], []
        for first_head in (True, False):
            qh = (jnp.where(lo, qb, 0.0) if first_head else jnp.where(lo, 0.0, qb)).astype(BF16)
            s = jnp.where(ok, _dot_nt(qh, kb), NEG_INF)
            m = jnp.max(s, axis=-1, keepdims=True)
            p = jnp.exp(s - m).astype(BF16)
            vh = (jnp.where(lo_k, vb, 1.0) if first_head else jnp.where(lo_k, 1.0, vb)).astype(BF16)
            ms.append(m)
            rs.append(_dot(p, vh))
        m_scr[br, qrows, :] = jnp.where(lo, ms[0], ms[1])
        a_scr[br, qrows, :] = jnp.where(lo, rs[0], rs[1])
        l_scr[br, qrows, :] = jnp.where(lo, rs[1], rs[0])

    for br, (window, dil) in enumerate(ATT_BRANCHES):
        nb = seq // dil // blk
        if nb == 1:
            def body(r, carry, br=br, dil=dil):
                attend(br, dil, r, r, blk, 0)
                return carry
            lax.fori_loop(0, dil, body, 0, unroll=2)
        else:
            def body(idx, carry, br=br, dil=dil):
                n = idx // dil
                r = idx - n * dil
                first = n == 0
                q_start = n * (blk * dil) + r
                k_start = jnp.where(first, 0, n - 1) * (blk * dil) + r
                attend(br, dil, q_start, k_start, 2 * blk, jnp.where(first, 0, blk))
                return carry
            lax.fori_loop(0, nb * dil, body, 0, unroll=2)

    chunk = 256

    def combine(c, carry):
        rows = pl.ds(pl.multiple_of(c * chunk, chunk), chunk)
        m = [m_scr[b, rows, :] for b in range(3)]
        mx = jnp.maximum(jnp.maximum(m[0], m[1]), m[2])
        den = jnp.zeros((chunk, LANES), F32)
        num = jnp.zeros((chunk, LANES), F32)
        for b in range(3):
            w = jnp.exp(m[b] - mx)
            den = den + w * pltpu.roll(l_scr[b, rows, :], HEAD_DIM, axis=1)
            num = num + w * a_scr[b, rows, :]
        o_ref[rows, :] = num / den
        return carry

    lax.fori_loop(0, seq // chunk, combine, 0)


def _attn_prompt(q, k, v, batch, seq):
    spec = pl.BlockSpec((seq, LANES), lambda b, hp: (b, hp))
    return pl.pallas_call(
        functools.partial(_attn_prompt_kernel, seq=seq),
        grid=(batch, ATT_WIDTH // LANES),
        in_specs=[spec, spec, spec],
        out_specs=spec,
        out_shape=jax.ShapeDtypeStruct((batch * seq, ATT_WIDTH), F32),
        scratch_shapes=[pltpu.VMEM((3, seq, LANES), F32)] * 3,
        compiler_params=_params("parallel", "parallel"),
        name="attn_prompt",
    )(q, k, v)


def _attn_sample_kernel(q_ref, kn_ref, vn_ref, kc_ref, vc_ref, o_ref, ko_ref, vo_ref,
                        *, n_new, n_buf):
    rows = q_ref.shape[2]
    q = q_ref[0, 0].astype(BF16)
    kc = kc_ref[0, 0]
    vc = vc_ref[0, 0]
    kn = kn_ref[0, 0]
    vn = vn_ref[0, 0]
    s_c = _dot(q, kc.astype(BF16))
    s_n = _dot(q, kn.astype(BF16))
    tok_c = lax.broadcasted_iota(jnp.int32, (rows, n_buf), 0)
    dist_c = n_buf + tok_c - lax.broadcasted_iota(jnp.int32, (rows, n_buf), 1)
    tok_n = lax.broadcasted_iota(jnp.int32, (rows, LANES), 0)
    key_n = lax.broadcasted_iota(jnp.int32, (rows, LANES), 1) - (LANES - n_new)
    dist_n = tok_n - key_n
    parts = []
    for window, dil in ATT_BRANCHES:
        ok_c = jnp.where((dist_c & (dil - 1)) == 0, dist_c, window + 1) <= window
        ok_n = lax.bitcast_convert_type(
            jnp.where((dist_n & (dil - 1)) == 0, jnp.where(key_n >= 0, dist_n, -1), -1),
            jnp.uint32) <= window
        sc = jnp.where(ok_c, s_c, NEG_INF)
        sn = jnp.where(ok_n, s_n, NEG_INF)
        m = jnp.maximum(jnp.max(sc, axis=-1, keepdims=True), jnp.max(sn, axis=-1, keepdims=True))
        pc = jnp.exp(sc - m)
        pn = jnp.exp(sn - m)
        l = jnp.sum(pc, axis=-1, keepdims=True) + jnp.sum(pn, axis=-1, keepdims=True)
        acc = (_dot_nt(pc.astype(BF16), vc.astype(BF16))
               + _dot_nt(pn.astype(BF16), vn.astype(BF16)))
        parts.append((m, l, acc))
    mx = jnp.maximum(jnp.maximum(parts[0][0], parts[1][0]), parts[2][0])
    den = jnp.zeros((rows, 1), F32)
    num = jnp.zeros((rows, HEAD_DIM), F32)
    for m, l, acc in parts:
        w = jnp.exp(m - mx)
        den = den + w * l
        num = num + w * acc
    o_ref[0, 0] = num / den
    fresh = lax.broadcasted_iota(jnp.int32, (HEAD_DIM, LANES), 1) >= LANES - n_new
    for c_val, n_val, out_ref in ((kc, kn, ko_ref), (vc, vn, vo_ref)):
        shifted = pltpu.roll(c_val, n_buf - n_new, axis=1)
        out_ref[0, 0, :, :n_buf - LANES] = shifted[:, :n_buf - LANES]
        out_ref[0, 0, :, n_buf - LANES:] = jnp.where(fresh, n_val, shifted[:, n_buf - LANES:])


def _attn_sample(q_pad, kn_t, vn_t, cache_kt, cache_vt, n_new):
    batch, heads, rows, _ = q_pad.shape
    n_buf = cache_kt.shape[3]
    idx = lambda b, h: (b, h, 0, 0)
    buf_spec = pl.BlockSpec((1, 1, HEAD_DIM, n_buf), idx)
    new_spec = pl.BlockSpec((1, 1, HEAD_DIM, LANES), idx)
    q_spec = pl.BlockSpec((1, 1, rows, HEAD_DIM), idx)
    buf_shape = jax.ShapeDtypeStruct((batch, heads, HEAD_DIM, n_buf), F32)
    return pl.pallas_call(
        functools.partial(_attn_sample_kernel, n_new=n_new, n_buf=n_buf),
        grid=(batch, heads),
        in_specs=[q_spec, new_spec, new_spec, buf_spec, buf_spec],
        out_specs=[q_spec, buf_spec, buf_spec],
        out_shape=[jax.ShapeDtypeStruct((batch, heads, rows, HEAD_DIM), F32), buf_shape, buf_shape],
        compiler_params=_params("parallel", "parallel"),
        name="attn_sample",
    )(q_pad, kn_t, vn_t, cache_kt, cache_vt)


def _hgrn_kernel(*refs, ts, n_valid, has_state, layer):
    if has_state:
        hq_ref, hf_ref, hi_ref, hg_ref, lbl_ref, nw_ref, s0_ref, o_ref, sfin_ref, st_scr = refs
    else:
        hq_ref, hf_ref, hi_ref, hg_ref, lbl_ref, nw_ref, o_ref, sfin_ref, st_scr = refs
    C = HGRN_CHUNK
    stretch = pl.program_id(1)

    @pl.when(stretch == 0)
    def _():
        for h in range(HGRN_HEADS):
            if has_state:
                st_scr[h] = s0_ref[0, h].T
            else:
                st_scr[h] = jnp.zeros((HGRN_DK, HGRN_DK), F32)

    lg = lbl_ref[...]
    ex = jnp.exp(lg - jnp.max(lg, axis=0, keepdims=True))
    lb_all = jnp.sum(ex[:layer + 1], axis=0, keepdims=True) / jnp.sum(ex, axis=0, keepdims=True)
    nw = nw_ref[...]
    rowi = lax.broadcasted_iota(jnp.int32, (C, HGRN_DK), 0)
    ones = jnp.ones((HGRN_DK, LANES), BF16)
    base = stretch * ts

    def one_head(h, r0, rows):
        cols = slice(h * HGRN_DK, (h + 1) * HGRN_DK)
        lb = lb_all[:, cols]
        f = lb + (1.0 - lb) * jax.nn.sigmoid(hf_ref[rows, cols])
        logf = jnp.log(f)
        kk = 1.0 - f
        if n_valid is not None:
            live = (rowi + (base + r0)) < n_valid
            logf = jnp.where(live, logf, 0.0)
            kk = jnp.where(live, kk, 0.0)
        q = jax.nn.silu(hq_ref[rows, cols])
        ii = hi_ref[rows, cols]
        b = logf
        sh = 1
        while sh < C:
            b = b + jnp.where(rowi >= sh, pltpu.roll(b, sh, axis=0), 0.0)
            sh *= 2
        st = st_scr[h]
        inter = _dot_nt((q * jnp.exp(b)).astype(BF16), st.astype(BF16))
        live_rows = [min(C, SUBLANES * (t // SUBLANES + 1)) for t in range(C)]
        prods = []
        for t, nr in enumerate(live_rows):
            e = jnp.exp(jnp.where(rowi[:nr] <= t, b[t:t + 1, :] - b[:nr], NEG_INF))
            prods.append(q[t:t + 1, :] * e * kk[:nr])
        a_rep = _dot(jnp.concatenate(prods, axis=0).astype(BF16), ones)
        intra, off = [], 0
        for nr in live_rows:
            intra.append(jnp.sum(a_rep[off:off + nr, :] * ii[:nr], axis=0, keepdims=True))
            off += nr
        o = inter + jnp.concatenate(intra, axis=0)
        b_last = b[C - 1:C, :]
        kdec = kk * jnp.exp(b_last - b)
        st_scr[h] = st * jnp.exp(b_last) + _dot_tn(ii.astype(BF16), kdec.astype(BF16))
        o_ref[rows, cols] = _rms(o, nw) * jax.nn.silu(hg_ref[rows, cols])

    def body(c, carry):
        r0 = pl.multiple_of(c * C, C)
        rows = pl.ds(r0, C)
        for h in range(HGRN_HEADS):
            one_head(h, r0, rows)
        return carry

    lax.fori_loop(0, ts // C, body, 0)

    @pl.when(stretch == pl.num_programs(1) - 1)
    def _():
        for h in range(HGRN_HEADS):
            sfin_ref[0, h] = st_scr[h].T


def _hgrn(g4, lb_logits, norm_w, state, batch, seq, ts, n_valid, layer):
    has_state = state is not None
    nh = HGRN_HEADS
    per_seq = seq // ts
    col = lambda part: pl.BlockSpec((ts, HGRN_WIDTH), lambda b, s, part=part: (b * per_seq + s, part))
    st_spec = pl.BlockSpec((1, nh, HGRN_DK, HGRN_DK), lambda b, s: (b, 0, 0, 0))
    in_specs = [col(0), col(1), col(2), col(3),
                pl.BlockSpec(lb_logits.shape, lambda b, s: (0, 0)),
                pl.BlockSpec((1, HGRN_DK), lambda b, s: (0, 0))]
    args = [g4, g4, g4, g4, lb_logits, norm_w]
    if has_state:
        in_specs.append(st_spec)
        args.append(state)
    return pl.pallas_call(
        functools.partial(_hgrn_kernel, ts=ts, n_valid=n_valid, has_state=has_state, layer=layer),
        grid=(batch, per_seq),
        in_specs=in_specs,
        out_specs=[pl.BlockSpec((ts, HGRN_WIDTH), lambda b, s: (b * per_seq + s, 0)), st_spec],
        out_shape=[jax.ShapeDtypeStruct((batch * seq, HGRN_WIDTH), F32),
                   jax.ShapeDtypeStruct((batch, nh, HGRN_DK, HGRN_DK), F32)],
        scratch_shapes=[pltpu.VMEM((nh, HGRN_DK, HGRN_DK), F32)],
        compiler_params=_params("parallel", "arbitrary"),
        name="hgrn",
    )(*args)


def _out_proj_kernel(x_ref, att_ref, og_ref, wo_ref, nw_ref, wq_ref, h_ref, xn_ref, qp_ref):
    h = (x_ref[...]
         + _dot(att_ref[...].astype(BF16), wo_ref[:ATT_WIDTH, :])
         + _dot(og_ref[...].astype(BF16), wo_ref[ATT_WIDTH:, :]))
    h_ref[...] = h
    xn = _rms(h, nw_ref[...]).astype(BF16)
    xn_ref[...] = xn
    qp_ref[...] = _dot(xn, wq_ref[...])


def _out_proj(x, att, og, w_out_b, norm_w, wq_b, tm):
    n = x.shape[0]
    qcols = wq_b.shape[1]
    row = lambda i: (i, 0)
    fixed = lambda i: (0, 0)
    return pl.pallas_call(
        _out_proj_kernel,
        grid=(n // tm,),
        in_specs=[pl.BlockSpec((tm, D_MODEL), row),
                  pl.BlockSpec((tm, ATT_WIDTH), row),
                  pl.BlockSpec((tm, HGRN_WIDTH), row),
                  pl.BlockSpec((ATT_WIDTH + HGRN_WIDTH, D_MODEL), fixed),
                  pl.BlockSpec((1, D_MODEL), fixed),
                  pl.BlockSpec((D_MODEL, qcols), fixed)],
        out_specs=[pl.BlockSpec((tm, D_MODEL), row),
                   pl.BlockSpec((tm, D_MODEL), row),
                   pl.BlockSpec((tm, qcols), row)],
        out_shape=[jax.ShapeDtypeStruct((n, D_MODEL), F32),
                   jax.ShapeDtypeStruct((n, D_MODEL), BF16),
                   jax.ShapeDtypeStruct((n, qcols), F32)],
        compiler_params=_params("parallel"),
        name="out_proj",
    )(x, att, og, w_out_b, norm_w, wq_b)


def _top_values(s, count):
    rows = lax.broadcasted_iota(jnp.int32, (count, s.shape[1]), 0)
    out = jnp.zeros((count, s.shape[1]), F32)
    cur = s
    for r in range(count):
        m = jnp.max(cur, axis=0, keepdims=True)
        out = jnp.where(rows == r, m, out)
        if r + 1 < count:
            cur = jnp.where(cur == m, NEG_INF, cur)
    return out


def _peer_scores_kernel(qp_ref, keys_ref, s2_ref, th_ref, e1_ref, e2_ref):
    tn = qp_ref.shape[0]
    k = PEER_TOPK
    half = PEER_DKEY // 2
    rowk = lax.broadcasted_iota(jnp.int32, (k, tn), 0)
    for h in range(PEER_HEADS):
        q1 = qp_ref[:, h * PEER_DKEY:h * PEER_DKEY + half].astype(BF16)
        q2 = qp_ref[:, h * PEER_DKEY + half:(h + 1) * PEER_DKEY].astype(BF16)
        s1 = _dot_nt(keys_ref[h, 0], q1)
        s2 = _dot_nt(keys_ref[h, 1], q2)
        t1 = _top_values(s1, k)
        t2 = _top_values(s2, k)
        cands = []
        for i in range(k // 2):
            cands.append(jnp.where(rowk < k // (i + 1), t1[i:i + 1, :] + t2, NEG_INF))
        cands.append(t1[k // 2:, :] + t2[0:1, :])
        cand = jnp.concatenate(cands, axis=0)
        top = t1[0:1, :] + t2[0:1, :]
        cur = cand
        for _ in range(k - 1):
            cur = jnp.where(cur == jnp.max(cur, axis=0, keepdims=True), NEG_INF, cur)
        tau = jnp.max(cur, axis=0, keepdims=True)
        z = jnp.sum(jnp.where(cand >= tau, jnp.exp(cand - top), 0.0), axis=0, keepdims=True)
        th = jnp.full((N_KEYS, tn), jnp.inf, F32)
        for j in range(k):
            t2j = t2[j:j + 1, :]
            th = jnp.where(s1 + t2j >= tau, t2j, th)
        s2_ref[h] = s2
        th_ref[h] = th
        e1_ref[h] = jnp.exp(s1 - t1[0:1, :])
        e2_ref[h] = jnp.exp(s2 - t2[0:1, :]) * (0.5 / z)


def _peer_scores(qp, keys_b, tn):
    n = qp.shape[0]
    out_spec = pl.BlockSpec((PEER_HEADS, N_KEYS, tn), lambda i: (0, 0, i))
    out_shape = jax.ShapeDtypeStruct((PEER_HEADS, N_KEYS, n), F32)
    return pl.pallas_call(
        _peer_scores_kernel,
        grid=(n // tn,),
        in_specs=[pl.BlockSpec((tn, PEER_HEADS * PEER_DKEY), lambda i: (i, 0)),
                  pl.BlockSpec((PEER_HEADS, 2, N_KEYS, PEER_DKEY // 2), lambda i: (0, 0, 0, 0))],
        out_specs=[out_spec] * 4,
        out_shape=[out_shape] * 4,
        compiler_params=_params("parallel"),
        name="peer_scores",
    )(qp, keys_b)


def _peer_kernel(x_ref, h_ref, u_ref, vt_ref, s2_ref, th_ref, e1_ref, e2_ref, nfw_ref,
                 o_ref, acc_scr, z_scr, c0_scr, c1_scr, *, eb, tn, nblk):
    j = pl.program_id(1)
    slabs = eb // N_KEYS
    assert slabs == SUBLANES
    parts = 4
    part_rows = eb // parts

    def coefficients(c_ref):
        arows = pl.ds(pl.multiple_of(j * slabs, slabs), slabs)
        for part in range(parts):
            prow = slice(part * part_rows, (part + 1) * part_rows)
            z_scr[prow, :] = _dot_nt(u_ref[prow, :], x_ref[...])
            for al in range(part * slabs // parts, (part + 1) * slabs // parts):
                erow = slice(al * N_KEYS, (al + 1) * N_KEYS)
                for tc in range(tn // LANES):
                    lanes = slice(tc * LANES, (tc + 1) * LANES)
                    w = jnp.zeros((N_KEYS, LANES), F32)
                    for h in range(PEER_HEADS):
                        th = th_ref[h, arows, lanes][al:al + 1, :]
                        e1 = e1_ref[h, arows, lanes][al:al + 1, :]
                        w = w + jnp.where(s2_ref[h, :, lanes] >= th, e2_ref[h, :, lanes], 0.0) * e1
                    z = z_scr[erow, lanes]
                    c_ref[erow, lanes] = (w * (z * (1.0 + lax.erf(z * math.sqrt(0.5))))).astype(BF16)

    def accumulate(c_ref):
        acc_scr[...] += _dot(vt_ref[...], c_ref[...])

    odd = (j % 2) == 1
    inner = jnp.logical_and(j > 0, j < nblk)

    @pl.when(j == 0)
    def _():
        acc_scr[...] = jnp.zeros_like(acc_scr)
        coefficients(c0_scr)

    @pl.when(jnp.logical_and(inner, odd))
    def _():
        accumulate(c0_scr)
        coefficients(c1_scr)

    @pl.when(jnp.logical_and(inner, jnp.logical_not(odd)))
    def _():
        accumulate(c1_scr)
        coefficients(c0_scr)

    @pl.when(j == nblk)
    def _():
        accumulate(c1_scr if (nblk - 1) % 2 == 1 else c0_scr)
        hh = h_ref[...] + acc_scr[...].T
        o_ref[...] = _rms(hh, nfw_ref[...])


def _peer(xn, h, u_b, vt_b, s2, th, e1, e2, nf_w, tn, eb):
    n = xn.shape[0]
    nblk = u_b.shape[0] // eb
    tok = pl.BlockSpec((tn, D_MODEL), lambda i, j: (i, 0))
    sc = pl.BlockSpec((PEER_HEADS, N_KEYS, tn), lambda i, j: (0, 0, i))
    return pl.pallas_call(
        functools.partial(_peer_kernel, eb=eb, tn=tn, nblk=nblk),
        grid=(n // tn, nblk + 1),
        in_specs=[tok, tok,
                  pl.BlockSpec((eb, D_MODEL), lambda i, j: (jnp.minimum(j, nblk - 1), 0)),
                  pl.BlockSpec((D_MODEL, eb), lambda i, j: (0, jnp.maximum(j - 1, 0))),
                  sc, sc, sc, sc,
                  pl.BlockSpec((1, D_MODEL), lambda i, j: (0, 0))],
        out_specs=tok,
        out_shape=jax.ShapeDtypeStruct((n, D_MODEL), F32),
        scratch_shapes=[pltpu.VMEM((D_MODEL, tn), F32),
                        pltpu.VMEM((eb, tn), F32),
                        pltpu.VMEM((eb, tn), BF16),
                        pltpu.VMEM((eb, tn), BF16)],
        compiler_params=_params("parallel", "arbitrary"),
        name="peer_experts",
    )(xn, h, u_b, vt_b, s2, th, e1, e2, nf_w)


def _tail(x, att, og, w, tm, tn, eb):
    h, xn, qp = _out_proj(x, att, og, w["w_out"], w["norm2"], w["wq"], tm)
    s2, th, e1, e2 = _peer_scores(qp, w["keys"], min(tn, 256))
    return _peer(xn, h, w["u"], w["vt"], s2, th, e1, e2, w["norm_f"], tn, eb)


def _window_view(t):
    return jnp.transpose(t, (0, 2, 3, 1))


def _window_unview(t):
    return jnp.transpose(t, (0, 3, 1, 2))


def kernel(x_prompt, x_sample, cache_k_win, cache_v_win, state_hgrn, norm1_w, w_in,
           hgrn_norm_w, hgrn_lb_logits, w_out, norm2_w, peer_wq, peer_sub_keys,
           peer_u, peer_v, norm_f_w):
    layer = 0
    batch, seq, _ = x_prompt.shape
    dec_batch, dec_seq, _ = x_sample.shape
    w_in_b = w_in[layer].astype(BF16)
    w = {
        "norm1": norm1_w[layer][None, :],
        "w_in": w_in_b,
        "w_kvt": w_in_b[:, ATT_WIDTH:3 * ATT_WIDTH].T,
        "hgrn_norm": hgrn_norm_w[layer][None, :],
        "w_out": w_out[layer].astype(BF16),
        "norm2": norm2_w[layer][None, :],
        "wq": peer_wq[layer].astype(BF16),
        "keys": peer_sub_keys[layer].astype(BF16),
        "u": peer_u[layer].astype(BF16),
        "vt": peer_v[layer].astype(BF16).T,
        "norm_f": norm_f_w[None, :],
    }
    heads = (ATT_HEADS, HEAD_DIM)

    xp = x_prompt.reshape(batch * seq, D_MODEL)
    q, k, v, g4, kt, vt = _in_proj(xp, w["norm1"], w["w_in"], w["w_kvt"], 256, seq)
    att = _attn_prompt(q, k, v, batch, seq)
    og, s_prompt = _hgrn(g4, hgrn_lb_logits, w["hgrn_norm"], None, batch, seq, seq // 2, None, layer)
    y_prompt = _tail(xp, att, og, w, 256, 512, PEER_EXPERT_BLOCK).reshape(batch, seq, D_MODEL)
    keep = min(max(wd for wd, _ in ATT_BRANCHES), seq)
    k_win_prompt = _window_unview(kt.reshape(batch, *heads, seq)[..., seq - keep:])
    v_win_prompt = _window_unview(vt.reshape(batch, *heads, seq)[..., seq - keep:])

    tms = dec_batch * dec_seq
    xs = x_sample.reshape(tms, D_MODEL)
    qs, _, _, g4s, kts, vts = _in_proj(xs, w["norm1"], w["w_in"], w["w_kvt"], tms, tms)
    rows = SUBLANES
    q_pad = jnp.pad(jnp.transpose(qs.reshape(dec_batch, dec_seq, *heads), (0, 2, 1, 3)),
                    ((0, 0), (0, 0), (0, rows - dec_seq), (0, 0)))
    new_t = lambda t: jnp.pad(
        jnp.transpose(t.reshape(*heads, dec_batch, dec_seq), (2, 0, 1, 3)),
        ((0, 0), (0, 0), (0, 0), (LANES - dec_seq, 0)))
    att_s, k_buf, v_buf = _attn_sample(q_pad, new_t(kts), new_t(vts),
                                       _window_view(cache_k_win[layer]),
                                       _window_view(cache_v_win[layer]), dec_seq)
    att_s = jnp.transpose(att_s[:, :, :dec_seq], (0, 2, 1, 3)).reshape(tms, ATT_WIDTH)
    seq_pad = HGRN_CHUNK
    g4p = jnp.pad(g4s.reshape(dec_batch, dec_seq, 4 * HGRN_WIDTH),
                  ((0, 0), (0, seq_pad - dec_seq), (0, 0))).reshape(dec_batch * seq_pad, 4 * HGRN_WIDTH)
    ogp, s_sample = _hgrn(g4p, hgrn_lb_logits, w["hgrn_norm"], state_hgrn[layer],
                          dec_batch, seq_pad, seq_pad, dec_seq, layer)
    og_s = ogp.reshape(dec_batch, seq_pad, HGRN_WIDTH)[:, :dec_seq].reshape(tms, HGRN_WIDTH)
    y_sample = _tail(xs, att_s, og_s, w, tms, tms, PEER_EXPERT_BLOCK).reshape(dec_batch, dec_seq, D_MODEL)

    return (y_prompt, y_sample,
            k_win_prompt[None], v_win_prompt[None],
            s_prompt[None],
            _window_unview(k_buf)[None], _window_unview(v_buf)[None],
            s_sample.astype(state_hgrn.dtype)[None])
```

```python
import functools
import math

import jax
import jax.numpy as jnp
from jax import lax
from jax.experimental import pallas as pl
from jax.experimental.pallas import tpu as pltpu

F32 = jnp.float32
BF16 = jnp.bfloat16

D_MODEL = 1024
HEAD_DIM = 64
ATT_WIDTH = 512
ATT_HEADS = 8
ATT_BRANCHES = ((128, 1), (512, 4), (2048, 16))
ATT_STEPS = 128
ATT_SCALE = HEAD_DIM ** -0.5
ATT_UNROLL = 8
HGRN_WIDTH = 512
HGRN_DK = 128
HGRN_HEADS = 4
HGRN_CHUNK = 16
IN_COLS = 3 * ATT_WIDTH + 4 * HGRN_WIDTH
N_KEYS = 128
PEER_HEADS = 8
PEER_DKEY = 256
PEER_TOPK = 16
EPS = 1e-6
NEG_INF = float("-inf")

LANES = 128
SUBLANES = 8
PEER_EXPERT_BLOCK = SUBLANES * N_KEYS
VMEM_LIMIT = 56 * 1024 * 1024


def _dot(a, b):
    return jnp.dot(a, b, preferred_element_type=F32)


def _dot_nt(a, b):
    return lax.dot_general(a, b, (((1,), (1,)), ((), ())), preferred_element_type=F32)


def _dot_tn(a, b):
    return lax.dot_general(a, b, (((0,), (0,)), ((), ())), preferred_element_type=F32)


def _rms(x, w):
    return x * lax.rsqrt(jnp.mean(x * x, axis=-1, keepdims=True) + EPS) * w


def _params(*sem):
    return pltpu.CompilerParams(dimension_semantics=sem, vmem_limit_bytes=VMEM_LIMIT)


def _in_proj_kernel(x_ref, nw_ref, w_ref, wkv_ref, q_ref, k_ref, v_ref, g_ref, kt_ref, vt_ref):
    xn = _rms(x_ref[...], nw_ref[...]).astype(BF16)
    p = _dot(xn, w_ref[...])
    q_ref[...] = p[:, :ATT_WIDTH] * ATT_SCALE
    k_ref[...] = p[:, ATT_WIDTH:2 * ATT_WIDTH]
    v_ref[...] = p[:, 2 * ATT_WIDTH:3 * ATT_WIDTH]
    g_ref[...] = p[:, 3 * ATT_WIDTH:]
    kvt = _dot_nt(wkv_ref[...], xn)
    kt_ref[0] = kvt[:ATT_WIDTH]
    vt_ref[0] = kvt[ATT_WIDTH:]


def _in_proj(x, norm_w, w_in_b, w_kvt_b, tm, seq):
    n = x.shape[0]
    per_seq = seq // tm
    row = lambda i: (i, 0)
    fixed = lambda i: (0, 0)
    tr = lambda i: (i // per_seq, 0, i % per_seq)
    return pl.pallas_call(
        _in_proj_kernel,
        grid=(n // tm,),
        in_specs=[pl.BlockSpec((tm, D_MODEL), row),
                  pl.BlockSpec((1, D_MODEL), fixed),
                  pl.BlockSpec((D_MODEL, IN_COLS), fixed),
                  pl.BlockSpec((2 * ATT_WIDTH, D_MODEL), fixed)],
        out_specs=[pl.BlockSpec((tm, ATT_WIDTH), row),
                   pl.BlockSpec((tm, ATT_WIDTH), row),
                   pl.BlockSpec((tm, ATT_WIDTH), row),
                   pl.BlockSpec((tm, 4 * HGRN_WIDTH), row),
                   pl.BlockSpec((1, ATT_WIDTH, tm), tr),
                   pl.BlockSpec((1, ATT_WIDTH, tm), tr)],
        out_shape=[jax.ShapeDtypeStruct((n, ATT_WIDTH), F32),
                   jax.ShapeDtypeStruct((n, ATT_WIDTH), F32),
                   jax.ShapeDtypeStruct((n, ATT_WIDTH), F32),
                   jax.ShapeDtypeStruct((n, 4 * HGRN_WIDTH), F32),
                   jax.ShapeDtypeStruct((n // seq, ATT_WIDTH, seq), F32),
                   jax.ShapeDtypeStruct((n // seq, ATT_WIDTH, seq), F32)],
        compiler_params=_params("parallel"),
        name="in_proj",
    )(x, norm_w, w_in_b, w_kvt_b)


def _attn_prompt_kernel(q_ref, k_ref, v_ref, o_ref, qc_scr, kc_scr, vc_scr, m_scr, l_scr, a_scr,
                        *, seq):
    blk = ATT_STEPS
    (_, d0), (_, d1), (_, d2) = ATT_BRANCHES
    assert d0 == 1 and d2 % d1 == 0
    cls = seq // d1
    sub = d2 // d1
    lo = lax.broadcasted_iota(jnp.int32, (blk, LANES), 1) < HEAD_DIM

    for r in range(d1):
        src = pl.ds(r, cls, stride=d1)
        dst = pl.ds(r * cls, cls)
        qc_scr[dst, :] = q_ref[src, :]
        kc_scr[dst, :] = k_ref[src, :]
        vc_scr[dst, :] = v_ref[src, :]

    def attend(srcs, br, stride, q_start, k_start, n_keys, off):
        q_src, k_src, v_src = srcs
        step = {} if stride == 1 else {"stride": stride}
        qrows = pl.ds(q_start, blk, **step)
        krows = pl.ds(k_start, n_keys, **step)
        qb = q_src[qrows, :]
        kb = k_src[krows, :].astype(BF16)
        vb = v_src[krows, :]
        dist = (off + lax.broadcasted_iota(jnp.int32, (blk, n_keys), 0)
                - lax.broadcasted_iota(jnp.int32, (blk, n_keys), 1))
        ok = lax.bitcast_convert_type(dist, jnp.uint32) <= blk
        lo_k = lax.broadcasted_iota(jnp.int32, (n_keys, LANES), 1) < HEAD_DIM
        ms, ps, vhs = [], [], []
        for first_head in (True, False):
            qh = (jnp.where(lo, qb, 0.0) if first_head else jnp.where(lo, 0.0, qb)).astype(BF16)
            s = jnp.where(ok, _dot_nt(qh, kb), NEG_INF)
            m = jnp.max(s, axis=-1, keepdims=True)
            ms.append(m)
            ps.append(jnp.exp(s - m).astype(BF16))
            vhs.append((jnp.where(lo_k, vb, 1.0) if first_head else jnp.where(lo_k, 1.0, vb)).astype(BF16))
        return qrows, ms, ps, vhs

    def finish(br, qrows, ms, ps, vhs):
        rs = [_dot(p, vh) for p, vh in zip(ps, vhs)]
        m_scr[br, qrows, :] = jnp.where(lo, ms[0], ms[1])
        a_scr[br, qrows, :] = jnp.where(lo, rs[0], rs[1])
        l_scr[br, qrows, :] = jnp.where(lo, rs[1], rs[0])

    def run_branch(srcs, br, stride, n_classes, class_base, class_len):
        nb = class_len // blk

        def one(idx):
            n = idx // n_classes
            c = idx - n * n_classes
            base = class_base(c)
            if nb == 1:
                return attend(srcs, br, stride, base, base, blk, 0)
            first = n == 0
            q_start = base + n * (blk * stride)
            k_start = base + jnp.where(first, 0, n - 1) * (blk * stride)
            return attend(srcs, br, stride, q_start, k_start, 2 * blk, jnp.where(first, 0, blk))

        def body(trip, carry):
            held = [one(trip * ATT_UNROLL + g) for g in range(ATT_UNROLL)]
            for parts in held:
                finish(br, *parts)
            return carry

        lax.fori_loop(0, nb * n_classes // ATT_UNROLL, body, 0)

    natural = (q_ref, k_ref, v_ref)
    by_class = (qc_scr, kc_scr, vc_scr)
    run_branch(natural, 0, 1, 1, lambda c: 0, seq)
    run_branch(by_class, 1, 1, d1, lambda c: c * cls, cls)
    run_branch(by_class, 2, sub, d2, lambda c: (c // sub) * cls + c % sub, cls // sub)

    chunk = 256
    per_class = cls // chunk

    def combine(idx, carry):
        r = idx // per_class
        c = idx - r * per_class
        nat = pl.ds(r + c * (chunk * d1), chunk, stride=d1)
        byc = pl.ds(pl.multiple_of(r * cls + c * chunk, chunk), chunk)
        rows = (nat, byc, byc)
        m = [m_scr[b, rows[b], :] for b in range(3)]
        mx = jnp.maximum(jnp.maximum(m[0], m[1]), m[2])
        den = jnp.zeros((chunk, LANES), F32)
        num = jnp.zeros((chunk, LANES), F32)
        for b in range(3):
            w = jnp.exp(m[b] - mx)
            den = den + w * pltpu.roll(l_scr[b, rows[b], :], HEAD_DIM, axis=1)
            num = num + w * a_scr[b, rows[b], :]
        o_ref[nat, :] = num / den
        return carry

    lax.fori_loop(0, d1 * per_class, combine, 0)


def _attn_prompt(q, k, v, batch, seq):
    spec = pl.BlockSpec((seq, LANES), lambda b, hp: (b, hp))
    return pl.pallas_call(
        functools.partial(_attn_prompt_kernel, seq=seq),
        grid=(batch, ATT_WIDTH // LANES),
        in_specs=[spec, spec, spec],
        out_specs=spec,
        out_shape=jax.ShapeDtypeStruct((batch * seq, ATT_WIDTH), F32),
        scratch_shapes=[pltpu.VMEM((seq, LANES), F32)] * 3 + [pltpu.VMEM((3, seq, LANES), F32)] * 3,
        compiler_params=_params("parallel", "parallel"),
        name="attn_prompt",
    )(q, k, v)


def _attn_sample_kernel(q_ref, kn_ref, vn_ref, kc_ref, vc_ref, o_ref, ko_ref, vo_ref,
                        *, n_new, n_buf):
    rows = q_ref.shape[2]
    q = q_ref[0, 0].astype(BF16)
    kc = kc_ref[0, 0]
    vc = vc_ref[0, 0]
    kn = kn_ref[0, 0]
    vn = vn_ref[0, 0]
    s_c = _dot(q, kc.astype(BF16))
    s_n = _dot(q, kn.astype(BF16))
    tok_c = lax.broadcasted_iota(jnp.int32, (rows, n_buf), 0)
    dist_c = n_buf + tok_c - lax.broadcasted_iota(jnp.int32, (rows, n_buf), 1)
    tok_n = lax.broadcasted_iota(jnp.int32, (rows, LANES), 0)
    key_n = lax.broadcasted_iota(jnp.int32, (rows, LANES), 1) - (LANES - n_new)
    dist_n = tok_n - key_n
    parts = []
    for window, dil in ATT_BRANCHES:
        ok_c = jnp.where((dist_c & (dil - 1)) == 0, dist_c, window + 1) <= window
        ok_n = lax.bitcast_convert_type(
            jnp.where((dist_n & (dil - 1)) == 0, jnp.where(key_n >= 0, dist_n, -1), -1),
            jnp.uint32) <= window
        sc = jnp.where(ok_c, s_c, NEG_INF)
        sn = jnp.where(ok_n, s_n, NEG_INF)
        m = jnp.maximum(jnp.max(sc, axis=-1, keepdims=True), jnp.max(sn, axis=-1, keepdims=True))
        pc = jnp.exp(sc - m)
        pn = jnp.exp(sn - m)
        l = jnp.sum(pc, axis=-1, keepdims=True) + jnp.sum(pn, axis=-1, keepdims=True)
        acc = (_dot_nt(pc.astype(BF16), vc.astype(BF16))
               + _dot_nt(pn.astype(BF16), vn.astype(BF16)))
        parts.append((m, l, acc))
    mx = jnp.maximum(jnp.maximum(parts[0][0], parts[1][0]), parts[2][0])
    den = jnp.zeros((rows, 1), F32)
    num = jnp.zeros((rows, HEAD_DIM), F32)
    for m, l, acc in parts:
        w = jnp.exp(m - mx)
        den = den + w * l
        num = num + w * acc
    o_ref[0, 0] = num / den
    fresh = lax.broadcasted_iota(jnp.int32, (HEAD_DIM, LANES), 1) >= LANES - n_new
    for c_val, n_val, out_ref in ((kc, kn, ko_ref), (vc, vn, vo_ref)):
        shifted = pltpu.roll(c_val, n_buf - n_new, axis=1)
        out_ref[0, 0, :, :n_buf - LANES] = shifted[:, :n_buf - LANES]
        out_ref[0, 0, :, n_buf - LANES:] = jnp.where(fresh, n_val, shifted[:, n_buf - LANES:])


def _attn_sample(q_pad, kn_t, vn_t, cache_kt, cache_vt, n_new):
    batch, heads, rows, _ = q_pad.shape
    n_buf = cache_kt.shape[3]
    idx = lambda b, h: (b, h, 0, 0)
    buf_spec = pl.BlockSpec((1, 1, HEAD_DIM, n_buf), idx)
    new_spec = pl.BlockSpec((1, 1, HEAD_DIM, LANES), idx)
    q_spec = pl.BlockSpec((1, 1, rows, HEAD_DIM), idx)
    buf_shape = jax.ShapeDtypeStruct((batch, heads, HEAD_DIM, n_buf), F32)
    return pl.pallas_call(
        functools.partial(_attn_sample_kernel, n_new=n_new, n_buf=n_buf),
        grid=(batch, heads),
        in_specs=[q_spec, new_spec, new_spec, buf_spec, buf_spec],
        out_specs=[q_spec, buf_spec, buf_spec],
        out_shape=[jax.ShapeDtypeStruct((batch, heads, rows, HEAD_DIM), F32), buf_shape, buf_shape],
        compiler_params=_params("parallel", "parallel"),
        name="attn_sample",
    )(q_pad, kn_t, vn_t, cache_kt, cache_vt)


def _hgrn_kernel(*refs, ts, n_valid, has_state, layer):
    if has_state:
        hq_ref, hf_ref, hi_ref, hg_ref, lbl_ref, nw_ref, s0_ref, o_ref, sfin_ref, st_scr = refs
    else:
        hq_ref, hf_ref, hi_ref, hg_ref, lbl_ref, nw_ref, o_ref, sfin_ref, st_scr = refs
    C = HGRN_CHUNK
    stretch = pl.program_id(1)

    @pl.when(stretch == 0)
    def _():
        for h in range(HGRN_HEADS):
            if has_state:
                st_scr[h] = s0_ref[0, h].T
            else:
                st_scr[h] = jnp.zeros((HGRN_DK, HGRN_DK), F32)

    lg = lbl_ref[...]
    ex = jnp.exp(lg - jnp.max(lg, axis=0, keepdims=True))
    lb_all = jnp.sum(ex[:layer + 1], axis=0, keepdims=True) / jnp.sum(ex, axis=0, keepdims=True)
    nw = nw_ref[...]
    rowi = lax.broadcasted_iota(jnp.int32, (C, HGRN_DK), 0)
    ones = jnp.ones((HGRN_DK, LANES), BF16)
    base = stretch * ts

    def one_head(h, r0, rows):
        cols = slice(h * HGRN_DK, (h + 1) * HGRN_DK)
        lb = lb_all[:, cols]
        f = lb + (1.0 - lb) * jax.nn.sigmoid(hf_ref[rows, cols])
        logf = jnp.log(f)
        kk = 1.0 - f
        if n_valid is not None:
            live = (rowi + (base + r0)) < n_valid
            logf = jnp.where(live, logf, 0.0)
            kk = jnp.where(live, kk, 0.0)
        q = jax.nn.silu(hq_ref[rows, cols])
        ii = hi_ref[rows, cols]
        b = logf
        sh = 1
        while sh < C:
            b = b + jnp.where(rowi >= sh, pltpu.roll(b, sh, axis=0), 0.0)
            sh *= 2
        st = st_scr[h]
        inter = _dot_nt((q * jnp.exp(b)).astype(BF16), st.astype(BF16))
        live_rows = [min(C, SUBLANES * (t // SUBLANES + 1)) for t in range(C)]
        prods = []
        for t, nr in enumerate(live_rows):
            e = jnp.exp(jnp.where(rowi[:nr] <= t, b[t:t + 1, :] - b[:nr], NEG_INF))
            prods.append(q[t:t + 1, :] * e * kk[:nr])
        a_rep = _dot(jnp.concatenate(prods, axis=0).astype(BF16), ones)
        intra, off = [], 0
        for nr in live_rows:
            intra.append(jnp.sum(a_rep[off:off + nr, :] * ii[:nr], axis=0, keepdims=True))
            off += nr
        o = inter + jnp.concatenate(intra, axis=0)
        b_last = b[C - 1:C, :]
        kdec = kk * jnp.exp(b_last - b)
        st_scr[h] = st * jnp.exp(b_last) + _dot_tn(ii.astype(BF16), kdec.astype(BF16))
        o_ref[rows, cols] = _rms(o, nw) * jax.nn.silu(hg_ref[rows, cols])

    def body(c, carry):
        r0 = pl.multiple_of(c * C, C)
        rows = pl.ds(r0, C)
        for h in range(HGRN_HEADS):
            one_head(h, r0, rows)
        return carry

    lax.fori_loop(0, ts // C, body, 0)

    @pl.when(stretch == pl.num_programs(1) - 1)
    def _():
        for h in range(HGRN_HEADS):
            sfin_ref[0, h] = st_scr[h].T


def _hgrn(g4, lb_logits, norm_w, state, batch, seq, ts, n_valid, layer):
    has_state = state is not None
    nh = HGRN_HEADS
    per_seq = seq // ts
    col = lambda part: pl.BlockSpec((ts, HGRN_WIDTH), lambda b, s, part=part: (b * per_seq + s, part))
    st_spec = pl.BlockSpec((1, nh, HGRN_DK, HGRN_DK), lambda b, s: (b, 0, 0, 0))
    in_specs = [col(0), col(1), col(2), col(3),
                pl.BlockSpec(lb_logits.shape, lambda b, s: (0, 0)),
                pl.BlockSpec((1, HGRN_DK), lambda b, s: (0, 0))]
    args = [g4, g4, g4, g4, lb_logits, norm_w]
    if has_state:
        in_specs.append(st_spec)
        args.append(state)
    return pl.pallas_call(
        functools.partial(_hgrn_kernel, ts=ts, n_valid=n_valid, has_state=has_state, layer=layer),
        grid=(batch, per_seq),
        in_specs=in_specs,
        out_specs=[pl.BlockSpec((ts, HGRN_WIDTH), lambda b, s: (b * per_seq + s, 0)), st_spec],
        out_shape=[jax.ShapeDtypeStruct((batch * seq, HGRN_WIDTH), F32),
                   jax.ShapeDtypeStruct((batch, nh, HGRN_DK, HGRN_DK), F32)],
        scratch_shapes=[pltpu.VMEM((nh, HGRN_DK, HGRN_DK), F32)],
        compiler_params=_params("parallel", "arbitrary"),
        name="hgrn",
    )(*args)


def _out_proj_kernel(x_ref, att_ref, og_ref, wo_ref, nw_ref, wq_ref, h_ref, xn_ref, qp_ref):
    h = (x_ref[...]
         + _dot(att_ref[...].astype(BF16), wo_ref[:ATT_WIDTH, :])
         + _dot(og_ref[...].astype(BF16), wo_ref[ATT_WIDTH:, :]))
    h_ref[...] = h
    xn = _rms(h, nw_ref[...]).astype(BF16)
    xn_ref[...] = xn
    qp_ref[...] = _dot(xn, wq_ref[...])


def _out_proj(x, att, og, w_out_b, norm_w, wq_b, tm):
    n = x.shape[0]
    qcols = wq_b.shape[1]
    row = lambda i: (i, 0)
    fixed = lambda i: (0, 0)
    return pl.pallas_call(
        _out_proj_kernel,
        grid=(n // tm,),
        in_specs=[pl.BlockSpec((tm, D_MODEL), row),
                  pl.BlockSpec((tm, ATT_WIDTH), row),
                  pl.BlockSpec((tm, HGRN_WIDTH), row),
                  pl.BlockSpec((ATT_WIDTH + HGRN_WIDTH, D_MODEL), fixed),
                  pl.BlockSpec((1, D_MODEL), fixed),
                  pl.BlockSpec((D_MODEL, qcols), fixed)],
        out_specs=[pl.BlockSpec((tm, D_MODEL), row),
                   pl.BlockSpec((tm, D_MODEL), row),
                   pl.BlockSpec((tm, qcols), row)],
        out_shape=[jax.ShapeDtypeStruct((n, D_MODEL), F32),
                   jax.ShapeDtypeStruct((n, D_MODEL), BF16),
                   jax.ShapeDtypeStruct((n, qcols), F32)],
        compiler_params=_params("parallel"),
        name="out_proj",
    )(x, att, og, w_out_b, norm_w, wq_b)


def _top_values(s, count):
    rows = lax.broadcasted_iota(jnp.int32, (count, s.shape[1]), 0)
    out = jnp.zeros((count, s.shape[1]), F32)
    cur = s
    for r in range(count):
        m = jnp.max(cur, axis=0, keepdims=True)
        out = jnp.where(rows == r, m, out)
        if r + 1 < count:
            cur = jnp.where(cur == m, NEG_INF, cur)
    return out


def _peer_scores_kernel(qp_ref, keys_ref, s2_ref, th_ref, e1_ref, e2_ref, th_scr, e1_scr):
    tn = qp_ref.shape[0]
    k = PEER_TOPK
    half = PEER_DKEY // 2
    rowk = lax.broadcasted_iota(jnp.int32, (k, tn), 0)
    for h in range(PEER_HEADS):
        q1 = qp_ref[:, h * PEER_DKEY:h * PEER_DKEY + half].astype(BF16)
        q2 = qp_ref[:, h * PEER_DKEY + half:(h + 1) * PEER_DKEY].astype(BF16)
        s1 = _dot_nt(keys_ref[h, 0], q1)
        s2 = _dot_nt(keys_ref[h, 1], q2)
        t1 = _top_values(s1, k)
        t2 = _top_values(s2, k)
        cands = []
        for i in range(k // 2):
            cands.append(jnp.where(rowk < k // (i + 1), t1[i:i + 1, :] + t2, NEG_INF))
        cands.append(t1[k // 2:, :] + t2[0:1, :])
        cand = jnp.concatenate(cands, axis=0)
        top = t1[0:1, :] + t2[0:1, :]
        cur = cand
        for _ in range(k - 1):
            cur = jnp.where(cur == jnp.max(cur, axis=0, keepdims=True), NEG_INF, cur)
        tau = jnp.max(cur, axis=0, keepdims=True)
        z = jnp.sum(jnp.where(cand >= tau, jnp.exp(cand - top), 0.0), axis=0, keepdims=True)
        th = jnp.full((N_KEYS, tn), jnp.inf, F32)
        for j in range(k):
            t2j = t2[j:j + 1, :]
            th = jnp.where(s1 + t2j >= tau, t2j, th)
        s2_ref[h] = s2
        th_scr[h] = th
        e1_scr[h] = jnp.exp(s1 - t1[0:1, :])
        e2_ref[h] = jnp.exp(s2 - t2[0:1, :]) * (0.5 / z)
    th_ref[...] = jnp.transpose(th_scr[...], (1, 0, 2))
    e1_ref[...] = jnp.transpose(e1_scr[...], (1, 0, 2))


def _peer_scores(qp, keys_b, tn):
    n = qp.shape[0]
    by_head = pl.BlockSpec((PEER_HEADS, N_KEYS, tn), lambda i: (0, 0, i))
    by_key = pl.BlockSpec((N_KEYS, PEER_HEADS, tn), lambda i: (0, 0, i))
    head_shape = jax.ShapeDtypeStruct((PEER_HEADS, N_KEYS, n), F32)
    key_shape = jax.ShapeDtypeStruct((N_KEYS, PEER_HEADS, n), F32)
    return pl.pallas_call(
        _peer_scores_kernel,
        grid=(n // tn,),
        in_specs=[pl.BlockSpec((tn, PEER_HEADS * PEER_DKEY), lambda i: (i, 0)),
                  pl.BlockSpec((PEER_HEADS, 2, N_KEYS, PEER_DKEY // 2), lambda i: (0, 0, 0, 0))],
        out_specs=[by_head, by_key, by_key, by_head],
        out_shape=[head_shape, key_shape, key_shape, head_shape],
        scratch_shapes=[pltpu.VMEM((PEER_HEADS, N_KEYS, tn), F32)] * 2,
        compiler_params=_params("parallel"),
        name="peer_scores",
    )(qp, keys_b)


def _peer_kernel(x_ref, h_ref, u_ref, vt_ref, s2_ref, th_ref, e1_ref, e2_ref, nfw_ref,
                 o_ref, acc_scr, z_scr, w_scr, c0_scr, c1_scr, *, eb, tn, nblk):
    j = pl.program_id(1)
    slabs = eb // N_KEYS

    def coefficients(c_ref):
        z_scr[...] = _dot_nt(u_ref[...], x_ref[...])
        zeros = [jnp.zeros((SUBLANES, LANES), F32)] * 2
        tile = 0
        for al in range(slabs):
            a = j * slabs + al
            for tc in range(tn // LANES):
                lanes = slice(tc * LANES, (tc + 1) * LANES)
                ths = [th_ref[a, h:h + 1, lanes] for h in range(PEER_HEADS)]
                e1s = [e1_ref[a, h:h + 1, lanes] for h in range(PEER_HEADS)]
                zero = zeros[tile % 2]
                for v in range(N_KEYS // SUBLANES):
                    brow = slice(v * SUBLANES, (v + 1) * SUBLANES)
                    w = zero
                    for h in range(PEER_HEADS):
                        w = w + jnp.where(s2_ref[h, brow, lanes] >= ths[h],
                                          e2_ref[h, brow, lanes], 0.0) * e1s[h]
                    if v == 0:
                        zeros[tile % 2] = pltpu.roll(pltpu.roll(w * 0.0, 1, axis=1), 1, axis=1)
                    w_scr[al * N_KEYS + v * SUBLANES:al * N_KEYS + (v + 1) * SUBLANES, lanes] = w
                tile += 1
        rows = 2 * SUBLANES
        for r in range(eb // rows):
            erow = slice(r * rows, (r + 1) * rows)
            z = z_scr[erow, :]
            c_ref[erow, :] = (w_scr[erow, :] * (z * (1.0 + lax.erf(z * math.sqrt(0.5))))).astype(BF16)

    def accumulate(c_ref):
        acc_scr[...] += _dot(vt_ref[...], c_ref[...])

    odd = (j % 2) == 1
    inner = jnp.logical_and(j > 0, j < nblk)

    @pl.when(j == 0)
    def _():
        acc_scr[...] = jnp.zeros_like(acc_scr)
        coefficients(c0_scr)

    @pl.when(jnp.logical_and(inner, odd))
    def _():
        accumulate(c0_scr)
        coefficients(c1_scr)

    @pl.when(jnp.logical_and(inner, jnp.logical_not(odd)))
    def _():
        accumulate(c1_scr)
        coefficients(c0_scr)

    @pl.when(j == nblk)
    def _():
        accumulate(c1_scr if (nblk - 1) % 2 == 1 else c0_scr)
        hh = h_ref[...] + acc_scr[...].T
        o_ref[...] = _rms(hh, nfw_ref[...])


def _peer(xn, h, u_b, vt_b, s2, th, e1, e2, nf_w, tn, eb):
    n = xn.shape[0]
    nblk = u_b.shape[0] // eb
    tok = pl.BlockSpec((tn, D_MODEL), lambda i, j: (i, 0))
    by_head = pl.BlockSpec((PEER_HEADS, N_KEYS, tn), lambda i, j: (0, 0, i))
    by_key = pl.BlockSpec((N_KEYS, PEER_HEADS, tn), lambda i, j: (0, 0, i))
    return pl.pallas_call(
        functools.partial(_peer_kernel, eb=eb, tn=tn, nblk=nblk),
        grid=(n // tn, nblk + 1),
        in_specs=[tok, tok,
                  pl.BlockSpec((eb, D_MODEL), lambda i, j: (jnp.minimum(j, nblk - 1), 0)),
                  pl.BlockSpec((D_MODEL, eb), lambda i, j: (0, jnp.maximum(j - 1, 0))),
                  by_head, by_key, by_key, by_head,
                  pl.BlockSpec((1, D_MODEL), lambda i, j: (0, 0))],
        out_specs=tok,
        out_shape=jax.ShapeDtypeStruct((n, D_MODEL), F32),
        scratch_shapes=[pltpu.VMEM((D_MODEL, tn), F32),
                        pltpu.VMEM((eb, tn), F32),
                        pltpu.VMEM((eb, tn), F32),
                        pltpu.VMEM((eb, tn), BF16),
                        pltpu.VMEM((eb, tn), BF16)],
        compiler_params=_params("parallel", "arbitrary"),
        name="peer_experts",
    )(xn, h, u_b, vt_b, s2, th, e1, e2, nf_w)


def _tail(x, att, og, w, tm, tn, eb):
    h, xn, qp = _out_proj(x, att, og, w["w_out"], w["norm2"], w["wq"], tm)
    s2, th, e1, e2 = _peer_scores(qp, w["keys"], min(tn, 256))
    return _peer(xn, h, w["u"], w["vt"], s2, th, e1, e2, w["norm_f"], tn, eb)


def _window_view(t):
    return jnp.transpose(t, (0, 2, 3, 1))


def _window_unview(t):
    return jnp.transpose(t, (0, 3, 1, 2))


def kernel(x_prompt, x_sample, cache_k_win, cache_v_win, state_hgrn, norm1_w, w_in,
           hgrn_norm_w, hgrn_lb_logits, w_out, norm2_w, peer_wq, peer_sub_keys,
           peer_u, peer_v, norm_f_w):
    layer = 0
    batch, seq, _ = x_prompt.shape
    dec_batch, dec_seq, _ = x_sample.shape
    w_in_b = w_in[layer].astype(BF16)
    w = {
        "norm1": norm1_w[layer][None, :],
        "w_in": w_in_b,
        "w_kvt": w_in_b[:, ATT_WIDTH:3 * ATT_WIDTH].T,
        "hgrn_norm": hgrn_norm_w[layer][None, :],
        "w_out": w_out[layer].astype(BF16),
        "norm2": norm2_w[layer][None, :],
        "wq": peer_wq[layer].astype(BF16),
        "keys": peer_sub_keys[layer].astype(BF16),
        "u": peer_u[layer].astype(BF16),
        "vt": peer_v[layer].astype(BF16).T,
        "norm_f": norm_f_w[None, :],
    }
    heads = (ATT_HEADS, HEAD_DIM)

    xp = x_prompt.reshape(batch * seq, D_MODEL)
    q, k, v, g4, kt, vt = _in_proj(xp, w["norm1"], w["w_in"], w["w_kvt"], 256, seq)
    att = _attn_prompt(q, k, v, batch, seq)
    og, s_prompt = _hgrn(g4, hgrn_lb_logits, w["hgrn_norm"], None, batch, seq, seq // 2, None, layer)
    y_prompt = _tail(xp, att, og, w, 256, 512, PEER_EXPERT_BLOCK).reshape(batch, seq, D_MODEL)
    keep = min(max(wd for wd, _ in ATT_BRANCHES), seq)
    k_win_prompt = _window_unview(kt.reshape(batch, *heads, seq)[..., seq - keep:])
    v_win_prompt = _window_unview(vt.reshape(batch, *heads, seq)[..., seq - keep:])

    tms = dec_batch * dec_seq
    xs = x_sample.reshape(tms, D_MODEL)
    qs, _, _, g4s, kts, vts = _in_proj(xs, w["norm1"], w["w_in"], w["w_kvt"], tms, tms)
    rows = SUBLANES
    q_pad = jnp.pad(jnp.transpose(qs.reshape(dec_batch, dec_seq, *heads), (0, 2, 1, 3)),
                    ((0, 0), (0, 0), (0, rows - dec_seq), (0, 0)))
    new_t = lambda t: jnp.pad(
        jnp.transpose(t.reshape(*heads, dec_batch, dec_seq), (2, 0, 1, 3)),
        ((0, 0), (0, 0), (0, 0), (LANES - dec_seq, 0)))
    att_s, k_buf, v_buf = _attn_sample(q_pad, new_t(kts), new_t(vts),
                                       _window_view(cache_k_win[layer]),
                                       _window_view(cache_v_win[layer]), dec_seq)
    att_s = jnp.transpose(att_s[:, :, :dec_seq], (0, 2, 1, 3)).reshape(tms, ATT_WIDTH)
    seq_pad = HGRN_CHUNK
    g4p = jnp.pad(g4s.reshape(dec_batch, dec_seq, 4 * HGRN_WIDTH),
                  ((0, 0), (0, seq_pad - dec_seq), (0, 0))).reshape(dec_batch * seq_pad, 4 * HGRN_WIDTH)
    ogp, s_sample = _hgrn(g4p, hgrn_lb_logits, w["hgrn_norm"], state_hgrn[layer],
                          dec_batch, seq_pad, seq_pad, dec_seq, layer)
    og_s = ogp.reshape(dec_batch, seq_pad, HGRN_WIDTH)[:, :dec_seq].reshape(tms, HGRN_WIDTH)
    y_sample = _tail(xs, att_s, og_s, w, tms, tms, PEER_EXPERT_BLOCK).reshape(dec_batch, dec_seq, D_MODEL)

    return (y_prompt, y_sample,
            k_win_prompt[None], v_win_prompt[None],
            s_prompt[None],
            _window_unview(k_buf)[None], _window_unview(v_buf)[None],
            s_sample.astype(state_hgrn.dtype)[None])
```

```python
import functools
import math

import jax
import jax.numpy as jnp
from jax import lax
from jax.experimental import pallas as pl
from jax.experimental.pallas import tpu as pltpu

F32 = jnp.float32
BF16 = jnp.bfloat16

D_MODEL = 1024
HEAD_DIM = 64
ATT_WIDTH = 512
ATT_HEADS = 8
ATT_BRANCHES = ((128, 1), (512, 4), (2048, 16))
ATT_STEPS = 128
ATT_SCALE = HEAD_DIM ** -0.5
ATT_UNROLL = 8
HGRN_WIDTH = 512
HGRN_DK = 128
HGRN_HEADS = 4
HGRN_CHUNK = 16
IN_COLS = 3 * ATT_WIDTH + 4 * HGRN_WIDTH
N_KEYS = 128
PEER_HEADS = 8
PEER_DKEY = 256
PEER_TOPK = 16
EPS = 1e-6
NEG_INF = float("-inf")

LANES = 128
SUBLANES = 8
PEER_EXPERT_BLOCK = SUBLANES * N_KEYS
VMEM_LIMIT = 56 * 1024 * 1024


def _dot(a, b):
    return jnp.dot(a, b, preferred_element_type=F32)


def _dot_nt(a, b):
    return lax.dot_general(a, b, (((1,), (1,)), ((), ())), preferred_element_type=F32)


def _dot_tn(a, b):
    return lax.dot_general(a, b, (((0,), (0,)), ((), ())), preferred_element_type=F32)


def _rms(x, w):
    return x * lax.rsqrt(jnp.mean(x * x, axis=-1, keepdims=True) + EPS) * w


def _params(*sem):
    return pltpu.CompilerParams(dimension_semantics=sem, vmem_limit_bytes=VMEM_LIMIT)


def _in_proj_kernel(x_ref, nw_ref, w_ref, wkv_ref, q_ref, k_ref, v_ref, g_ref, kt_ref, vt_ref):
    xn = _rms(x_ref[...], nw_ref[...]).astype(BF16)
    p = _dot(xn, w_ref[...])
    q_ref[...] = p[:, :ATT_WIDTH] * ATT_SCALE
    k_ref[...] = p[:, ATT_WIDTH:2 * ATT_WIDTH]
    v_ref[...] = p[:, 2 * ATT_WIDTH:3 * ATT_WIDTH]
    g_ref[...] = p[:, 3 * ATT_WIDTH:]
    kvt = _dot_nt(wkv_ref[...], xn)
    kt_ref[0] = kvt[:ATT_WIDTH]
    vt_ref[0] = kvt[ATT_WIDTH:]


def _in_proj(x, norm_w, w_in_b, w_kvt_b, tm, seq):
    n = x.shape[0]
    per_seq = seq // tm
    row = lambda i: (i, 0)
    fixed = lambda i: (0, 0)
    tr = lambda i: (i // per_seq, 0, i % per_seq)
    return pl.pallas_call(
        _in_proj_kernel,
        grid=(n // tm,),
        in_specs=[pl.BlockSpec((tm, D_MODEL), row),
                  pl.BlockSpec((1, D_MODEL), fixed),
                  pl.BlockSpec((D_MODEL, IN_COLS), fixed),
                  pl.BlockSpec((2 * ATT_WIDTH, D_MODEL), fixed)],
        out_specs=[pl.BlockSpec((tm, ATT_WIDTH), row),
                   pl.BlockSpec((tm, ATT_WIDTH), row),
                   pl.BlockSpec((tm, ATT_WIDTH), row),
                   pl.BlockSpec((tm, 4 * HGRN_WIDTH), row),
                   pl.BlockSpec((1, ATT_WIDTH, tm), tr),
                   pl.BlockSpec((1, ATT_WIDTH, tm), tr)],
        out_shape=[jax.ShapeDtypeStruct((n, ATT_WIDTH), F32),
                   jax.ShapeDtypeStruct((n, ATT_WIDTH), F32),
                   jax.ShapeDtypeStruct((n, ATT_WIDTH), F32),
                   jax.ShapeDtypeStruct((n, 4 * HGRN_WIDTH), F32),
                   jax.ShapeDtypeStruct((n // seq, ATT_WIDTH, seq), F32),
                   jax.ShapeDtypeStruct((n // seq, ATT_WIDTH, seq), F32)],
        compiler_params=_params("parallel"),
        name="in_proj",
    )(x, norm_w, w_in_b, w_kvt_b)


def _attn_prompt_kernel(q_ref, k_ref, v_ref, o_ref, qc_scr, kc_scr, vc_scr, m_scr, l_scr, a_scr,
                        *, seq):
    blk = ATT_STEPS
    (_, d0), (_, d1), (_, d2) = ATT_BRANCHES
    assert d0 == 1 and d2 % d1 == 0
    cls = seq // d1
    sub = d2 // d1
    lo = lax.broadcasted_iota(jnp.int32, (blk, LANES), 1) < HEAD_DIM

    for r in range(d1):
        src = pl.ds(r, cls, stride=d1)
        dst = pl.ds(r * cls, cls)
        qc_scr[dst, :] = q_ref[src, :]
        kc_scr[dst, :] = k_ref[src, :]
        vc_scr[dst, :] = v_ref[src, :]

    def attend(srcs, br, stride, q_start, k_start, n_keys, off):
        q_src, k_src, v_src = srcs
        step = {} if stride == 1 else {"stride": stride}
        qrows = pl.ds(q_start, blk, **step)
        krows = pl.ds(k_start, n_keys, **step)
        qb = q_src[qrows, :]
        kb = k_src[krows, :].astype(BF16)
        vb = v_src[krows, :]
        dist = (off + lax.broadcasted_iota(jnp.int32, (blk, n_keys), 0)
                - lax.broadcasted_iota(jnp.int32, (blk, n_keys), 1))
        ok = lax.bitcast_convert_type(dist, jnp.uint32) <= blk
        lo_k = lax.broadcasted_iota(jnp.int32, (n_keys, LANES), 1) < HEAD_DIM
        ms, ps, vhs = [], [], []
        for first_head in (True, False):
            qh = (jnp.where(lo, qb, 0.0) if first_head else jnp.where(lo, 0.0, qb)).astype(BF16)
            s = jnp.where(ok, _dot_nt(qh, kb), NEG_INF)
            m = jnp.max(s, axis=-1, keepdims=True)
            ms.append(m)
            ps.append(jnp.exp(s - m).astype(BF16))
            vhs.append((jnp.where(lo_k, vb, 1.0) if first_head else jnp.where(lo_k, 1.0, vb)).astype(BF16))
        return qrows, ms, ps, vhs

    def finish(br, qrows, ms, ps, vhs):
        rs = [_dot(p, vh) for p, vh in zip(ps, vhs)]
        m_scr[br, qrows, :] = jnp.where(lo, ms[0], ms[1])
        a_scr[br, qrows, :] = jnp.where(lo, rs[0], rs[1])
        l_scr[br, qrows, :] = jnp.where(lo, rs[1], rs[0])

    def run_branch(srcs, br, stride, n_classes, class_base, class_len):
        nb = class_len // blk

        def one(idx):
            n = idx // n_classes
            c = idx - n * n_classes
            base = class_base(c)
            if nb == 1:
                return attend(srcs, br, stride, base, base, blk, 0)
            first = n == 0
            q_start = base + n * (blk * stride)
            k_start = base + jnp.where(first, 0, n - 1) * (blk * stride)
            return attend(srcs, br, stride, q_start, k_start, 2 * blk, jnp.where(first, 0, blk))

        def body(trip, carry):
            held = [one(trip * ATT_UNROLL + g) for g in range(ATT_UNROLL)]
            for parts in held:
                finish(br, *parts)
            return carry

        lax.fori_loop(0, nb * n_classes // ATT_UNROLL, body, 0)

    natural = (q_ref, k_ref, v_ref)
    by_class = (qc_scr, kc_scr, vc_scr)
    run_branch(natural, 0, 1, 1, lambda c: 0, seq)
    run_branch(by_class, 1, 1, d1, lambda c: c * cls, cls)
    run_branch(by_class, 2, sub, d2, lambda c: (c // sub) * cls + c % sub, cls // sub)

    chunk = 256
    per_class = cls // chunk

    def combine(idx, carry):
        r = idx // per_class
        c = idx - r * per_class
        nat = pl.ds(r + c * (chunk * d1), chunk, stride=d1)
        byc = pl.ds(pl.multiple_of(r * cls + c * chunk, chunk), chunk)
        rows = (nat, byc, byc)
        m = [m_scr[b, rows[b], :] for b in range(3)]
        mx = jnp.maximum(jnp.maximum(m[0], m[1]), m[2])
        den = jnp.zeros((chunk, LANES), F32)
        num = jnp.zeros((chunk, LANES), F32)
        for b in range(3):
            w = jnp.exp(m[b] - mx)
            den = den + w * pltpu.roll(l_scr[b, rows[b], :], HEAD_DIM, axis=1)
            num = num + w * a_scr[b, rows[b], :]
        o_ref[nat, :] = num / den
        return carry

    lax.fori_loop(0, d1 * per_class, combine, 0)


def _attn_prompt(q, k, v, batch, seq):
    spec = pl.BlockSpec((seq, LANES), lambda b, hp: (b, hp))
    return pl.pallas_call(
        functools.partial(_attn_prompt_kernel, seq=seq),
        grid=(batch, ATT_WIDTH // LANES),
        in_specs=[spec, spec, spec],
        out_specs=spec,
        out_shape=jax.ShapeDtypeStruct((batch * seq, ATT_WIDTH), F32),
        scratch_shapes=[pltpu.VMEM((seq, LANES), F32)] * 3 + [pltpu.VMEM((3, seq, LANES), F32)] * 3,
        compiler_params=_params("parallel", "parallel"),
        name="attn_prompt",
    )(q, k, v)


def _attn_sample_kernel(q_ref, kn_ref, vn_ref, kc_ref, vc_ref, o_ref, ko_ref, vo_ref,
                        *, n_new, n_buf):
    rows = q_ref.shape[2]
    q = q_ref[0, 0].astype(BF16)
    kc = kc_ref[0, 0]
    vc = vc_ref[0, 0]
    kn = kn_ref[0, 0]
    vn = vn_ref[0, 0]
    s_c = _dot(q, kc.astype(BF16))
    s_n = _dot(q, kn.astype(BF16))
    tok_c = lax.broadcasted_iota(jnp.int32, (rows, n_buf), 0)
    dist_c = n_buf + tok_c - lax.broadcasted_iota(jnp.int32, (rows, n_buf), 1)
    tok_n = lax.broadcasted_iota(jnp.int32, (rows, LANES), 0)
    key_n = lax.broadcasted_iota(jnp.int32, (rows, LANES), 1) - (LANES - n_new)
    dist_n = tok_n - key_n
    parts = []
    for window, dil in ATT_BRANCHES:
        ok_c = jnp.where((dist_c & (dil - 1)) == 0, dist_c, window + 1) <= window
        ok_n = lax.bitcast_convert_type(
            jnp.where((dist_n & (dil - 1)) == 0, jnp.where(key_n >= 0, dist_n, -1), -1),
            jnp.uint32) <= window
        sc = jnp.where(ok_c, s_c, NEG_INF)
        sn = jnp.where(ok_n, s_n, NEG_INF)
        m = jnp.maximum(jnp.max(sc, axis=-1, keepdims=True), jnp.max(sn, axis=-1, keepdims=True))
        pc = jnp.exp(sc - m)
        pn = jnp.exp(sn - m)
        l = jnp.sum(pc, axis=-1, keepdims=True) + jnp.sum(pn, axis=-1, keepdims=True)
        acc = (_dot_nt(pc.astype(BF16), vc.astype(BF16))
               + _dot_nt(pn.astype(BF16), vn.astype(BF16)))
        parts.append((m, l, acc))
    mx = jnp.maximum(jnp.maximum(parts[0][0], parts[1][0]), parts[2][0])
    den = jnp.zeros((rows, 1), F32)
    num = jnp.zeros((rows, HEAD_DIM), F32)
    for m, l, acc in parts:
        w = jnp.exp(m - mx)
        den = den + w * l
        num = num + w * acc
    o_ref[0, 0] = num / den
    fresh = lax.broadcasted_iota(jnp.int32, (HEAD_DIM, LANES), 1) >= LANES - n_new
    for c_val, n_val, out_ref in ((kc, kn, ko_ref), (vc, vn, vo_ref)):
        shifted = pltpu.roll(c_val, n_buf - n_new, axis=1)
        out_ref[0, 0, :, :n_buf - LANES] = shifted[:, :n_buf - LANES]
        out_ref[0, 0, :, n_buf - LANES:] = jnp.where(fresh, n_val, shifted[:, n_buf - LANES:])


def _attn_sample(q_pad, kn_t, vn_t, cache_kt, cache_vt, n_new):
    batch, heads, rows, _ = q_pad.shape
    n_buf = cache_kt.shape[3]
    idx = lambda b, h: (b, h, 0, 0)
    buf_spec = pl.BlockSpec((1, 1, HEAD_DIM, n_buf), idx)
    new_spec = pl.BlockSpec((1, 1, HEAD_DIM, LANES), idx)
    q_spec = pl.BlockSpec((1, 1, rows, HEAD_DIM), idx)
    buf_shape = jax.ShapeDtypeStruct((batch, heads, HEAD_DIM, n_buf), F32)
    return pl.pallas_call(
        functools.partial(_attn_sample_kernel, n_new=n_new, n_buf=n_buf),
        grid=(batch, heads),
        in_specs=[q_spec, new_spec, new_spec, buf_spec, buf_spec],
        out_specs=[q_spec, buf_spec, buf_spec],
        out_shape=[jax.ShapeDtypeStruct((batch, heads, rows, HEAD_DIM), F32), buf_shape, buf_shape],
        compiler_params=_params("parallel", "parallel"),
        name="attn_sample",
    )(q_pad, kn_t, vn_t, cache_kt, cache_vt)


def _hgrn_kernel(*refs, ts, n_valid, has_state, layer):
    if has_state:
        hq_ref, hf_ref, hi_ref, hg_ref, lbl_ref, nw_ref, s0_ref, o_ref, sfin_ref, st_scr = refs
    else:
        hq_ref, hf_ref, hi_ref, hg_ref, lbl_ref, nw_ref, o_ref, sfin_ref, st_scr = refs
    C = HGRN_CHUNK
    stretch = pl.program_id(1)

    @pl.when(stretch == 0)
    def _():
        for h in range(HGRN_HEADS):
            if has_state:
                st_scr[h] = s0_ref[0, h].T
            else:
                st_scr[h] = jnp.zeros((HGRN_DK, HGRN_DK), F32)

    lg = lbl_ref[...]
    ex = jnp.exp(lg - jnp.max(lg, axis=0, keepdims=True))
    lb_all = jnp.sum(ex[:layer + 1], axis=0, keepdims=True) / jnp.sum(ex, axis=0, keepdims=True)
    nw = nw_ref[...]
    rowi = lax.broadcasted_iota(jnp.int32, (C, HGRN_DK), 0)
    ones = jnp.ones((HGRN_DK, LANES), BF16)
    base = stretch * ts

    def one_head(h, r0, rows):
        cols = slice(h * HGRN_DK, (h + 1) * HGRN_DK)
        lb = lb_all[:, cols]
        f = lb + (1.0 - lb) * jax.nn.sigmoid(hf_ref[rows, cols])
        logf = jnp.log(f)
        kk = 1.0 - f
        if n_valid is not None:
            live = (rowi + (base + r0)) < n_valid
            logf = jnp.where(live, logf, 0.0)
            kk = jnp.where(live, kk, 0.0)
        q = jax.nn.silu(hq_ref[rows, cols])
        ii = hi_ref[rows, cols]
        b = logf
        sh = 1
        while sh < C:
            b = b + jnp.where(rowi >= sh, pltpu.roll(b, sh, axis=0), 0.0)
            sh *= 2
        st = st_scr[h]
        inter = _dot_nt((q * jnp.exp(b)).astype(BF16), st.astype(BF16))
        live_rows = [min(C, SUBLANES * (t // SUBLANES + 1)) for t in range(C)]
        prods = []
        for t, nr in enumerate(live_rows):
            e = jnp.exp(jnp.where(rowi[:nr] <= t, b[t:t + 1, :] - b[:nr], NEG_INF))
            prods.append(q[t:t + 1, :] * e * kk[:nr])
        a_rep = _dot(jnp.concatenate(prods, axis=0).astype(BF16), ones)
        intra, off = [], 0
        for nr in live_rows:
            intra.append(jnp.sum(a_rep[off:off + nr, :] * ii[:nr], axis=0, keepdims=True))
            off += nr
        o = inter + jnp.concatenate(intra, axis=0)
        b_last = b[C - 1:C, :]
        kdec = kk * jnp.exp(b_last - b)
        st_scr[h] = st * jnp.exp(b_last) + _dot_tn(ii.astype(BF16), kdec.astype(BF16))
        o_ref[rows, cols] = _rms(o, nw) * jax.nn.silu(hg_ref[rows, cols])

    def body(c, carry):
        r0 = pl.multiple_of(c * C, C)
        rows = pl.ds(r0, C)
        for h in range(HGRN_HEADS):
            one_head(h, r0, rows)
        return carry

    lax.fori_loop(0, ts // C, body, 0)

    @pl.when(stretch == pl.num_programs(1) - 1)
    def _():
        for h in range(HGRN_HEADS):
            sfin_ref[0, h] = st_scr[h].T


def _hgrn(g4, lb_logits, norm_w, state, batch, seq, ts, n_valid, layer):
    has_state = state is not None
    nh = HGRN_HEADS
    per_seq = seq // ts
    col = lambda part: pl.BlockSpec((ts, HGRN_WIDTH), lambda b, s, part=part: (b * per_seq + s, part))
    st_spec = pl.BlockSpec((1, nh, HGRN_DK, HGRN_DK), lambda b, s: (b, 0, 0, 0))
    in_specs = [col(0), col(1), col(2), col(3),
                pl.BlockSpec(lb_logits.shape, lambda b, s: (0, 0)),
                pl.BlockSpec((1, HGRN_DK), lambda b, s: (0, 0))]
    args = [g4, g4, g4, g4, lb_logits, norm_w]
    if has_state:
        in_specs.append(st_spec)
        args.append(state)
    return pl.pallas_call(
        functools.partial(_hgrn_kernel, ts=ts, n_valid=n_valid, has_state=has_state, layer=layer),
        grid=(batch, per_seq),
        in_specs=in_specs,
        out_specs=[pl.BlockSpec((ts, HGRN_WIDTH), lambda b, s: (b * per_seq + s, 0)), st_spec],
        out_shape=[jax.ShapeDtypeStruct((batch * seq, HGRN_WIDTH), F32),
                   jax.ShapeDtypeStruct((batch, nh, HGRN_DK, HGRN_DK), F32)],
        scratch_shapes=[pltpu.VMEM((nh, HGRN_DK, HGRN_DK), F32)],
        compiler_params=_params("parallel", "arbitrary"),
        name="hgrn",
    )(*args)


def _out_proj_kernel(x_ref, att_ref, og_ref, wo_ref, nw_ref, wq_ref, h_ref, xn_ref, qp_ref):
    h = (x_ref[...]
         + _dot(att_ref[...].astype(BF16), wo_ref[:ATT_WIDTH, :])
         + _dot(og_ref[...].astype(BF16), wo_ref[ATT_WIDTH:, :]))
    h_ref[...] = h
    xn = _rms(h, nw_ref[...]).astype(BF16)
    xn_ref[...] = xn
    qp_ref[...] = _dot(xn, wq_ref[...])


def _out_proj(x, att, og, w_out_b, norm_w, wq_b, tm):
    n = x.shape[0]
    qcols = wq_b.shape[1]
    row = lambda i: (i, 0)
    fixed = lambda i: (0, 0)
    return pl.pallas_call(
        _out_proj_kernel,
        grid=(n // tm,),
        in_specs=[pl.BlockSpec((tm, D_MODEL), row),
                  pl.BlockSpec((tm, ATT_WIDTH), row),
                  pl.BlockSpec((tm, HGRN_WIDTH), row),
                  pl.BlockSpec((ATT_WIDTH + HGRN_WIDTH, D_MODEL), fixed),
                  pl.BlockSpec((1, D_MODEL), fixed),
                  pl.BlockSpec((D_MODEL, qcols), fixed)],
        out_specs=[pl.BlockSpec((tm, D_MODEL), row),
                   pl.BlockSpec((tm, D_MODEL), row),
                   pl.BlockSpec((tm, qcols), row)],
        out_shape=[jax.ShapeDtypeStruct((n, D_MODEL), F32),
                   jax.ShapeDtypeStruct((n, D_MODEL), BF16),
                   jax.ShapeDtypeStruct((n, qcols), F32)],
        compiler_params=_params("parallel"),
        name="out_proj",
    )(x, att, og, w_out_b, norm_w, wq_b)


def _top_values(s, count):
    rows = lax.broadcasted_iota(jnp.int32, (count, s.shape[1]), 0)
    out = jnp.zeros((count, s.shape[1]), F32)
    cur = s
    for r in range(count):
        m = jnp.max(cur, axis=0, keepdims=True)
        out = jnp.where(rows == r, m, out)
        if r + 1 < count:
            cur = jnp.where(cur == m, NEG_INF, cur)
    return out


def _peer_scores_kernel(qp_ref, keys_ref, rk_ref, cnt_ref, e1_ref, e2_ref, cnt_scr, e1_scr):
    tn = qp_ref.shape[0]
    k = PEER_TOPK
    half = PEER_DKEY // 2
    rowk = lax.broadcasted_iota(jnp.int32, (k, tn), 0)
    for h in range(PEER_HEADS):
        q1 = qp_ref[:, h * PEER_DKEY:h * PEER_DKEY + half].astype(BF16)
        q2 = qp_ref[:, h * PEER_DKEY + half:(h + 1) * PEER_DKEY].astype(BF16)
        s1 = _dot_nt(keys_ref[h, 0], q1)
        s2 = _dot_nt(keys_ref[h, 1], q2)
        t1 = _top_values(s1, k)
        t2 = _top_values(s2, k)
        cands = []
        for i in range(k // 2):
            cands.append(jnp.where(rowk < k // (i + 1), t1[i:i + 1, :] + t2, NEG_INF))
        cands.append(t1[k // 2:, :] + t2[0:1, :])
        cand = jnp.concatenate(cands, axis=0)
        top = t1[0:1, :] + t2[0:1, :]
        cur = cand
        for _ in range(k - 1):
            cur = jnp.where(cur == jnp.max(cur, axis=0, keepdims=True), NEG_INF, cur)
        tau = jnp.max(cur, axis=0, keepdims=True)
        z = jnp.sum(jnp.where(cand >= tau, jnp.exp(cand - top), 0.0), axis=0, keepdims=True)
        cnt = jnp.zeros((N_KEYS, tn), F32)
        for j in range(k):
            cnt = jnp.where(s1 + t2[j:j + 1, :] >= tau, float(j + 1), cnt)
        rank = jnp.full((N_KEYS, tn), float(k), F32)
        for j in reversed(range(k)):
            rank = jnp.where(s2 >= t2[j:j + 1, :], float(j), rank)
        rk_ref[h] = rank.astype(BF16)
        cnt_scr[h] = cnt
        e1_scr[h] = jnp.exp(s1 - t1[0:1, :])
        e2_ref[h] = (jnp.exp(s2 - t2[0:1, :]) * (0.5 / z)).astype(BF16)
    cnt_ref[...] = jnp.transpose(cnt_scr[...], (1, 0, 2))
    e1_ref[...] = jnp.transpose(e1_scr[...], (1, 0, 2))


def _peer_scores(qp, keys_b, tn):
    n = qp.shape[0]
    by_head = pl.BlockSpec((PEER_HEADS, N_KEYS, tn), lambda i: (0, 0, i))
    by_key = pl.BlockSpec((N_KEYS, PEER_HEADS, tn), lambda i: (0, 0, i))
    head_shape = jax.ShapeDtypeStruct((PEER_HEADS, N_KEYS, n), BF16)
    key_shape = jax.ShapeDtypeStruct((N_KEYS, PEER_HEADS, n), F32)
    return pl.pallas_call(
        _peer_scores_kernel,
        grid=(n // tn,),
        in_specs=[pl.BlockSpec((tn, PEER_HEADS * PEER_DKEY), lambda i: (i, 0)),
                  pl.BlockSpec((PEER_HEADS, 2, N_KEYS, PEER_DKEY // 2), lambda i: (0, 0, 0, 0))],
        out_specs=[by_head, by_key, by_key, by_head],
        out_shape=[head_shape, key_shape, key_shape, head_shape],
        scratch_shapes=[pltpu.VMEM((PEER_HEADS, N_KEYS, tn), F32)] * 2,
        compiler_params=_params("parallel"),
        name="peer_scores",
    )(qp, keys_b)


def _peer_kernel(x_ref, h_ref, u_ref, vt_ref, rk_in, cnt_ref, e1_ref, e2_in, nfw_ref,
                 o_ref, acc_scr, z_scr, w_scr, c0_scr, c1_scr, rk_ref, e2_ref, *, eb, tn, nblk):
    j = pl.program_id(1)
    slabs = eb // N_KEYS
    rows = 2 * SUBLANES

    def coefficients(c_ref):
        z_scr[...] = _dot_nt(u_ref[...], x_ref[...])
        zeros = [jnp.zeros((rows, LANES), BF16)] * 2
        tile = 0
        for al in range(slabs):
            a = j * slabs + al
            for tc in range(tn // LANES):
                lanes = slice(tc * LANES, (tc + 1) * LANES)
                cnts = [cnt_ref[a, h:h + 1, lanes].astype(BF16) for h in range(PEER_HEADS)]
                e1s = [e1_ref[a, h:h + 1, lanes].astype(BF16) for h in range(PEER_HEADS)]
                zero = zeros[tile % 2]
                for v in range(N_KEYS // rows):
                    brow = slice(v * rows, (v + 1) * rows)
                    w = zero
                    for h in range(PEER_HEADS):
                        w = w + jnp.where(rk_ref[h, brow, lanes] < cnts[h],
                                          e2_ref[h, brow, lanes], 0.0) * e1s[h]
                    if v == 0:
                        nil = pltpu.roll(pltpu.roll(w.astype(F32) * 0.0, 1, axis=1), 1, axis=1)
                        zeros[tile % 2] = nil.astype(BF16)
                    w_scr[al * N_KEYS + v * rows:al * N_KEYS + (v + 1) * rows, lanes] = w
                tile += 1
        for r in range(eb // rows):
            erow = slice(r * rows, (r + 1) * rows)
            z = z_scr[erow, :]
            c_ref[erow, :] = w_scr[erow, :] * (z * (1.0 + lax.erf(z * math.sqrt(0.5)))).astype(BF16)

    def accumulate(c_ref):
        acc_scr[...] += _dot(vt_ref[...], c_ref[...])

    odd = (j % 2) == 1
    inner = jnp.logical_and(j > 0, j < nblk)

    @pl.when(j == 0)
    def _():
        rk_ref[...] = rk_in[...]
        e2_ref[...] = e2_in[...]
        acc_scr[...] = jnp.zeros_like(acc_scr)
        coefficients(c0_scr)

    @pl.when(jnp.logical_and(inner, odd))
    def _():
        accumulate(c0_scr)
        coefficients(c1_scr)

    @pl.when(jnp.logical_and(inner, jnp.logical_not(odd)))
    def _():
        accumulate(c1_scr)
        coefficients(c0_scr)

    @pl.when(j == nblk)
    def _():
        accumulate(c1_scr if (nblk - 1) % 2 == 1 else c0_scr)
        hh = h_ref[...] + acc_scr[...].T
        o_ref[...] = _rms(hh, nfw_ref[...])


def _peer(xn, h, u_b, vt_b, rk, cnt, e1, e2, nf_w, tn, eb):
    n = xn.shape[0]
    nblk = u_b.shape[0] // eb
    tok = pl.BlockSpec((tn, D_MODEL), lambda i, j: (i, 0))
    by_head = pl.BlockSpec((PEER_HEADS, N_KEYS, tn), lambda i, j: (0, 0, i))
    by_key = pl.BlockSpec((N_KEYS, PEER_HEADS, tn), lambda i, j: (0, 0, i))
    return pl.pallas_call(
        functools.partial(_peer_kernel, eb=eb, tn=tn, nblk=nblk),
        grid=(n // tn, nblk + 1),
        in_specs=[tok, tok,
                  pl.BlockSpec((eb, D_MODEL), lambda i, j: (jnp.minimum(j, nblk - 1), 0)),
                  pl.BlockSpec((D_MODEL, eb), lambda i, j: (0, jnp.maximum(j - 1, 0))),
                  by_head, by_key, by_key, by_head,
                  pl.BlockSpec((1, D_MODEL), lambda i, j: (0, 0))],
        out_specs=tok,
        out_shape=jax.ShapeDtypeStruct((n, D_MODEL), F32),
        scratch_shapes=[pltpu.VMEM((D_MODEL, tn), F32),
                        pltpu.VMEM((eb, tn), F32),
                        pltpu.VMEM((eb, tn), BF16),
                        pltpu.VMEM((eb, tn), BF16),
                        pltpu.VMEM((eb, tn), BF16),
                        pltpu.VMEM((PEER_HEADS, N_KEYS, tn), BF16),
                        pltpu.VMEM((PEER_HEADS, N_KEYS, tn), BF16)],
        compiler_params=_params("parallel", "arbitrary"),
        name="peer_experts",
    )(xn, h, u_b, vt_b, rk, cnt, e1, e2, nf_w)


def _tail(x, att, og, w, tm, tn, eb):
    h, xn, qp = _out_proj(x, att, og, w["w_out"], w["norm2"], w["wq"], tm)
    rk, cnt, e1, e2 = _peer_scores(qp, w["keys"], min(tn, 256))
    return _peer(xn, h, w["u"], w["vt"], rk, cnt, e1, e2, w["norm_f"], tn, eb)


def _window_view(t):
    return jnp.transpose(t, (0, 2, 3, 1))


def _window_unview(t):
    return jnp.transpose(t, (0, 3, 1, 2))


def kernel(x_prompt, x_sample, cache_k_win, cache_v_win, state_hgrn, norm1_w, w_in,
           hgrn_norm_w, hgrn_lb_logits, w_out, norm2_w, peer_wq, peer_sub_keys,
           peer_u, peer_v, norm_f_w):
    layer = 0
    batch, seq, _ = x_prompt.shape
    dec_batch, dec_seq, _ = x_sample.shape
    w_in_b = w_in[layer].astype(BF16)
    w = {
        "norm1": norm1_w[layer][None, :],
        "w_in": w_in_b,
        "w_kvt": w_in_b[:, ATT_WIDTH:3 * ATT_WIDTH].T,
        "hgrn_norm": hgrn_norm_w[layer][None, :],
        "w_out": w_out[layer].astype(BF16),
        "norm2": norm2_w[layer][None, :],
        "wq": peer_wq[layer].astype(BF16),
        "keys": peer_sub_keys[layer].astype(BF16),
        "u": peer_u[layer].astype(BF16),
        "vt": peer_v[layer].astype(BF16).T,
        "norm_f": norm_f_w[None, :],
    }
    heads = (ATT_HEADS, HEAD_DIM)

    xp = x_prompt.reshape(batch * seq, D_MODEL)
    q, k, v, g4, kt, vt = _in_proj(xp, w["norm1"], w["w_in"], w["w_kvt"], 256, seq)
    att = _attn_prompt(q, k, v, batch, seq)
    og, s_prompt = _hgrn(g4, hgrn_lb_logits, w["hgrn_norm"], None, batch, seq, seq // 2, None, layer)
    y_prompt = _tail(xp, att, og, w, 256, 512, PEER_EXPERT_BLOCK).reshape(batch, seq, D_MODEL)
    keep = min(max(wd for wd, _ in ATT_BRANCHES), seq)
    k_win_prompt = _window_unview(kt.reshape(batch, *heads, seq)[..., seq - keep:])
    v_win_prompt = _window_unview(vt.reshape(batch, *heads, seq)[..., seq - keep:])

    tms = dec_batch * dec_seq
    xs = x_sample.reshape(tms, D_MODEL)
    qs, _, _, g4s, kts, vts = _in_proj(xs, w["norm1"], w["w_in"], w["w_kvt"], tms, tms)
    rows = SUBLANES
    q_pad = jnp.pad(jnp.transpose(qs.reshape(dec_batch, dec_seq, *heads), (0, 2, 1, 3)),
                    ((0, 0), (0, 0), (0, rows - dec_seq), (0, 0)))
    new_t = lambda t: jnp.pad(
        jnp.transpose(t.reshape(*heads, dec_batch, dec_seq), (2, 0, 1, 3)),
        ((0, 0), (0, 0), (0, 0), (LANES - dec_seq, 0)))
    att_s, k_buf, v_buf = _attn_sample(q_pad, new_t(kts), new_t(vts),
                                       _window_view(cache_k_win[layer]),
                                       _window_view(cache_v_win[layer]), dec_seq)
    att_s = jnp.transpose(att_s[:, :, :dec_seq], (0, 2, 1, 3)).reshape(tms, ATT_WIDTH)
    seq_pad = HGRN_CHUNK
    g4p = jnp.pad(g4s.reshape(dec_batch, dec_seq, 4 * HGRN_WIDTH),
                  ((0, 0), (0, seq_pad - dec_seq), (0, 0))).reshape(dec_batch * seq_pad, 4 * HGRN_WIDTH)
    ogp, s_sample = _hgrn(g4p, hgrn_lb_logits, w["hgrn_norm"], state_hgrn[layer],
                          dec_batch, seq_pad, seq_pad, dec_seq, layer)
    og_s = ogp.reshape(dec_batch, seq_pad, HGRN_WIDTH)[:, :dec_seq].reshape(tms, HGRN_WIDTH)
    y_sample = _tail(xs, att_s, og_s, w, tms, tms, PEER_EXPERT_BLOCK).reshape(dec_batch, dec_seq, D_MODEL)

    return (y_prompt, y_sample,
            k_win_prompt[None], v_win_prompt[None],
            s_prompt[None],
            _window_unview(k_buf)[None], _window_unview(v_buf)[None],
            s_sample.astype(state_hgrn.dtype)[None])
```

```python
import functools
import math

import jax
import jax.numpy as jnp
from jax import lax
from jax.experimental import pallas as pl
from jax.experimental.pallas import tpu as pltpu

F32 = jnp.float32
BF16 = jnp.bfloat16

D_MODEL = 1024
HEAD_DIM = 64
ATT_WIDTH = 512
ATT_HEADS = 8
ATT_BRANCHES = ((128, 1), (512, 4), (2048, 16))
ATT_STEPS = 128
ATT_SCALE = HEAD_DIM ** -0.5
ATT_UNROLL = 8
HGRN_WIDTH = 512
HGRN_DK = 128
HGRN_HEADS = 4
HGRN_CHUNK = 16
IN_COLS = 3 * ATT_WIDTH + 4 * HGRN_WIDTH
N_KEYS = 128
PEER_HEADS = 8
PEER_DKEY = 256
PEER_TOPK = 16
EPS = 1e-6
NEG_INF = float("-inf")

LANES = 128
SUBLANES = 8
PEER_EXPERT_BLOCK = SUBLANES * N_KEYS
VMEM_LIMIT = 56 * 1024 * 1024


def _dot(a, b):
    return jnp.dot(a, b, preferred_element_type=F32)


def _dot_nt(a, b):
    return lax.dot_general(a, b, (((1,), (1,)), ((), ())), preferred_element_type=F32)


def _dot_tn(a, b):
    return lax.dot_general(a, b, (((0,), (0,)), ((), ())), preferred_element_type=F32)


def _rms(x, w):
    return x * lax.rsqrt(jnp.mean(x * x, axis=-1, keepdims=True) + EPS) * w


def _params(*sem):
    return pltpu.CompilerParams(dimension_semantics=sem, vmem_limit_bytes=VMEM_LIMIT)


def _in_proj_kernel(x_ref, nw_ref, w_ref, wkv_ref, q_ref, k_ref, v_ref, g_ref, kt_ref, vt_ref):
    xn = _rms(x_ref[...], nw_ref[...]).astype(BF16)
    p = _dot(xn, w_ref[...])
    q_ref[...] = p[:, :ATT_WIDTH] * ATT_SCALE
    k_ref[...] = p[:, ATT_WIDTH:2 * ATT_WIDTH]
    v_ref[...] = p[:, 2 * ATT_WIDTH:3 * ATT_WIDTH]
    g_ref[...] = p[:, 3 * ATT_WIDTH:]
    kvt = _dot_nt(wkv_ref[...], xn)
    kt_ref[0] = kvt[:ATT_WIDTH]
    vt_ref[0] = kvt[ATT_WIDTH:]


def _in_proj(x, norm_w, w_in_b, w_kvt_b, tm, seq):
    n = x.shape[0]
    per_seq = seq // tm
    row = lambda i: (i, 0)
    fixed = lambda i: (0, 0)
    tr = lambda i: (i // per_seq, 0, i % per_seq)
    return pl.pallas_call(
        _in_proj_kernel,
        grid=(n // tm,),
        in_specs=[pl.BlockSpec((tm, D_MODEL), row),
                  pl.BlockSpec((1, D_MODEL), fixed),
                  pl.BlockSpec((D_MODEL, IN_COLS), fixed),
                  pl.BlockSpec((2 * ATT_WIDTH, D_MODEL), fixed)],
        out_specs=[pl.BlockSpec((tm, ATT_WIDTH), row),
                   pl.BlockSpec((tm, ATT_WIDTH), row),
                   pl.BlockSpec((tm, ATT_WIDTH), row),
                   pl.BlockSpec((tm, 4 * HGRN_WIDTH), row),
                   pl.BlockSpec((1, ATT_WIDTH, tm), tr),
                   pl.BlockSpec((1, ATT_WIDTH, tm), tr)],
        out_shape=[jax.ShapeDtypeStruct((n, ATT_WIDTH), F32),
                   jax.ShapeDtypeStruct((n, ATT_WIDTH), F32),
                   jax.ShapeDtypeStruct((n, ATT_WIDTH), F32),
                   jax.ShapeDtypeStruct((n, 4 * HGRN_WIDTH), F32),
                   jax.ShapeDtypeStruct((n // seq, ATT_WIDTH, seq), F32),
                   jax.ShapeDtypeStruct((n // seq, ATT_WIDTH, seq), F32)],
        compiler_params=_params("parallel"),
        name="in_proj",
    )(x, norm_w, w_in_b, w_kvt_b)


def _attn_prompt_kernel(q_ref, k_ref, v_ref, o_ref, qc_scr, kc_scr, vc_scr, m_scr, l_scr, a_scr,
                        *, seq):
    blk = ATT_STEPS
    (_, d0), (_, d1), (_, d2) = ATT_BRANCHES
    assert d0 == 1 and d2 % d1 == 0
    cls = seq // d1
    sub = d2 // d1
    lo = lax.broadcasted_iota(jnp.int32, (blk, LANES), 1) < HEAD_DIM

    for r in range(d1):
        src = pl.ds(r, cls, stride=d1)
        dst = pl.ds(r * cls, cls)
        qc_scr[dst, :] = q_ref[src, :]
        kc_scr[dst, :] = k_ref[src, :]
        vc_scr[dst, :] = v_ref[src, :]

    def attend(srcs, br, stride, q_start, k_start, n_keys, off):
        q_src, k_src, v_src = srcs
        step = {} if stride == 1 else {"stride": stride}
        qrows = pl.ds(q_start, blk, **step)
        krows = pl.ds(k_start, n_keys, **step)
        qb = q_src[qrows, :]
        kb = k_src[krows, :].astype(BF16)
        vb = v_src[krows, :]
        dist = (off + lax.broadcasted_iota(jnp.int32, (blk, n_keys), 0)
                - lax.broadcasted_iota(jnp.int32, (blk, n_keys), 1))
        ok = lax.bitcast_convert_type(dist, jnp.uint32) <= blk
        lo_k = lax.broadcasted_iota(jnp.int32, (n_keys, LANES), 1) < HEAD_DIM
        ms, ps, vhs = [], [], []
        for first_head in (True, False):
            qh = (jnp.where(lo, qb, 0.0) if first_head else jnp.where(lo, 0.0, qb)).astype(BF16)
            s = jnp.where(ok, _dot_nt(qh, kb), NEG_INF)
            m = jnp.max(s, axis=-1, keepdims=True)
            ms.append(m)
            ps.append(jnp.exp(s - m).astype(BF16))
            vhs.append((jnp.where(lo_k, vb, 1.0) if first_head else jnp.where(lo_k, 1.0, vb)).astype(BF16))
        return qrows, ms, ps, vhs

    def finish(br, qrows, ms, ps, vhs):
        rs = [_dot(p, vh) for p, vh in zip(ps, vhs)]
        m_scr[br, qrows, :] = jnp.where(lo, ms[0], ms[1])
        a_scr[br, qrows, :] = jnp.where(lo, rs[0], rs[1])
        l_scr[br, qrows, :] = jnp.where(lo, rs[1], rs[0])

    def run_branch(srcs, br, stride, n_classes, class_base, class_len):
        nb = class_len // blk

        def one(idx):
            n = idx // n_classes
            c = idx - n * n_classes
            base = class_base(c)
            if nb == 1:
                return attend(srcs, br, stride, base, base, blk, 0)
            first = n == 0
            q_start = base + n * (blk * stride)
            k_start = base + jnp.where(first, 0, n - 1) * (blk * stride)
            return attend(srcs, br, stride, q_start, k_start, 2 * blk, jnp.where(first, 0, blk))

        def body(trip, carry):
            held = [one(trip * ATT_UNROLL + g) for g in range(ATT_UNROLL)]
            for parts in held:
                finish(br, *parts)
            return carry

        lax.fori_loop(0, nb * n_classes // ATT_UNROLL, body, 0)

    natural = (q_ref, k_ref, v_ref)
    by_class = (qc_scr, kc_scr, vc_scr)
    run_branch(natural, 0, 1, 1, lambda c: 0, seq)
    run_branch(by_class, 1, 1, d1, lambda c: c * cls, cls)
    run_branch(by_class, 2, sub, d2, lambda c: (c // sub) * cls + c % sub, cls // sub)

    chunk = 256
    per_class = cls // chunk

    def combine(idx, carry):
        r = idx // per_class
        c = idx - r * per_class
        nat = pl.ds(r + c * (chunk * d1), chunk, stride=d1)
        byc = pl.ds(pl.multiple_of(r * cls + c * chunk, chunk), chunk)
        rows = (nat, byc, byc)
        m = [m_scr[b, rows[b], :] for b in range(3)]
        mx = jnp.maximum(jnp.maximum(m[0], m[1]), m[2])
        den = jnp.zeros((chunk, LANES), F32)
        num = jnp.zeros((chunk, LANES), F32)
        for b in range(3):
            w = jnp.exp(m[b] - mx)
            den = den + w * pltpu.roll(l_scr[b, rows[b], :], HEAD_DIM, axis=1)
            num = num + w * a_scr[b, rows[b], :]
        o_ref[nat, :] = num / den
        return carry

    lax.fori_loop(0, d1 * per_class, combine, 0)


def _attn_prompt(q, k, v, batch, seq):
    spec = pl.BlockSpec((seq, LANES), lambda b, hp: (b, hp))
    return pl.pallas_call(
        functools.partial(_attn_prompt_kernel, seq=seq),
        grid=(batch, ATT_WIDTH // LANES),
        in_specs=[spec, spec, spec],
        out_specs=spec,
        out_shape=jax.ShapeDtypeStruct((batch * seq, ATT_WIDTH), F32),
        scratch_shapes=[pltpu.VMEM((seq, LANES), F32)] * 3 + [pltpu.VMEM((3, seq, LANES), F32)] * 3,
        compiler_params=_params("parallel", "parallel"),
        name="attn_prompt",
    )(q, k, v)


def _attn_sample_kernel(q_ref, kn_ref, vn_ref, kc_ref, vc_ref, o_ref, ko_ref, vo_ref,
                        *, n_new, n_buf):
    rows = q_ref.shape[2]
    q = q_ref[0, 0].astype(BF16)
    kc = kc_ref[0, 0]
    vc = vc_ref[0, 0]
    kn = kn_ref[0, 0]
    vn = vn_ref[0, 0]
    s_c = _dot(q, kc.astype(BF16))
    s_n = _dot(q, kn.astype(BF16))
    tok_c = lax.broadcasted_iota(jnp.int32, (rows, n_buf), 0)
    dist_c = n_buf + tok_c - lax.broadcasted_iota(jnp.int32, (rows, n_buf), 1)
    tok_n = lax.broadcasted_iota(jnp.int32, (rows, LANES), 0)
    key_n = lax.broadcasted_iota(jnp.int32, (rows, LANES), 1) - (LANES - n_new)
    dist_n = tok_n - key_n
    parts = []
    for window, dil in ATT_BRANCHES:
        ok_c = jnp.where((dist_c & (dil - 1)) == 0, dist_c, window + 1) <= window
        ok_n = lax.bitcast_convert_type(
            jnp.where((dist_n & (dil - 1)) == 0, jnp.where(key_n >= 0, dist_n, -1), -1),
            jnp.uint32) <= window
        sc = jnp.where(ok_c, s_c, NEG_INF)
        sn = jnp.where(ok_n, s_n, NEG_INF)
        m = jnp.maximum(jnp.max(sc, axis=-1, keepdims=True), jnp.max(sn, axis=-1, keepdims=True))
        pc = jnp.exp(sc - m)
        pn = jnp.exp(sn - m)
        l = jnp.sum(pc, axis=-1, keepdims=True) + jnp.sum(pn, axis=-1, keepdims=True)
        acc = (_dot_nt(pc.astype(BF16), vc.astype(BF16))
               + _dot_nt(pn.astype(BF16), vn.astype(BF16)))
        parts.append((m, l, acc))
    mx = jnp.maximum(jnp.maximum(parts[0][0], parts[1][0]), parts[2][0])
    den = jnp.zeros((rows, 1), F32)
    num = jnp.zeros((rows, HEAD_DIM), F32)
    for m, l, acc in parts:
        w = jnp.exp(m - mx)
        den = den + w * l
        num = num + w * acc
    o_ref[0, 0] = num / den
    fresh = lax.broadcasted_iota(jnp.int32, (HEAD_DIM, LANES), 1) >= LANES - n_new
    for c_val, n_val, out_ref in ((kc, kn, ko_ref), (vc, vn, vo_ref)):
        shifted = pltpu.roll(c_val, n_buf - n_new, axis=1)
        out_ref[0, 0, :, :n_buf - LANES] = shifted[:, :n_buf - LANES]
        out_ref[0, 0, :, n_buf - LANES:] = jnp.where(fresh, n_val, shifted[:, n_buf - LANES:])


def _attn_sample(q_pad, kn_t, vn_t, cache_kt, cache_vt, n_new):
    batch, heads, rows, _ = q_pad.shape
    n_buf = cache_kt.shape[3]
    idx = lambda b, h: (b, h, 0, 0)
    buf_spec = pl.BlockSpec((1, 1, HEAD_DIM, n_buf), idx)
    new_spec = pl.BlockSpec((1, 1, HEAD_DIM, LANES), idx)
    q_spec = pl.BlockSpec((1, 1, rows, HEAD_DIM), idx)
    buf_shape = jax.ShapeDtypeStruct((batch, heads, HEAD_DIM, n_buf), F32)
    return pl.pallas_call(
        functools.partial(_attn_sample_kernel, n_new=n_new, n_buf=n_buf),
        grid=(batch, heads),
        in_specs=[q_spec, new_spec, new_spec, buf_spec, buf_spec],
        out_specs=[q_spec, buf_spec, buf_spec],
        out_shape=[jax.ShapeDtypeStruct((batch, heads, rows, HEAD_DIM), F32), buf_shape, buf_shape],
        compiler_params=_params("parallel", "parallel"),
        name="attn_sample",
    )(q_pad, kn_t, vn_t, cache_kt, cache_vt)


def _hgrn_kernel(*refs, ts, n_valid, has_state, layer):
    if has_state:
        hq_ref, hf_ref, hi_ref, hg_ref, lbl_ref, nw_ref, s0_ref, o_ref, sfin_ref, st_scr = refs
    else:
        hq_ref, hf_ref, hi_ref, hg_ref, lbl_ref, nw_ref, o_ref, sfin_ref, st_scr = refs
    C = HGRN_CHUNK
    stretch = pl.program_id(1)

    @pl.when(stretch == 0)
    def _():
        for h in range(HGRN_HEADS):
            if has_state:
                st_scr[h] = s0_ref[0, h].T
            else:
                st_scr[h] = jnp.zeros((HGRN_DK, HGRN_DK), F32)

    lg = lbl_ref[...]
    ex = jnp.exp(lg - jnp.max(lg, axis=0, keepdims=True))
    lb_all = jnp.sum(ex[:layer + 1], axis=0, keepdims=True) / jnp.sum(ex, axis=0, keepdims=True)
    nw = nw_ref[...]
    rowi = lax.broadcasted_iota(jnp.int32, (C, HGRN_DK), 0)
    ones = jnp.ones((HGRN_DK, LANES), BF16)
    base = stretch * ts

    def one_head(h, r0, rows):
        cols = slice(h * HGRN_DK, (h + 1) * HGRN_DK)
        lb = lb_all[:, cols]
        f = lb + (1.0 - lb) * jax.nn.sigmoid(hf_ref[rows, cols])
        logf = jnp.log(f)
        kk = 1.0 - f
        if n_valid is not None:
            live = (rowi + (base + r0)) < n_valid
            logf = jnp.where(live, logf, 0.0)
            kk = jnp.where(live, kk, 0.0)
        q = jax.nn.silu(hq_ref[rows, cols])
        ii = hi_ref[rows, cols]
        b = logf
        sh = 1
        while sh < C:
            b = b + jnp.where(rowi >= sh, pltpu.roll(b, sh, axis=0), 0.0)
            sh *= 2
        st = st_scr[h]
        inter = _dot_nt((q * jnp.exp(b)).astype(BF16), st.astype(BF16))
        live_rows = [min(C, SUBLANES * (t // SUBLANES + 1)) for t in range(C)]
        prods = []
        for t, nr in enumerate(live_rows):
            e = jnp.exp(jnp.where(rowi[:nr] <= t, b[t:t + 1, :] - b[:nr], NEG_INF))
            prods.append(q[t:t + 1, :] * e * kk[:nr])
        a_rep = _dot(jnp.concatenate(prods, axis=0).astype(BF16), ones)
        intra, off = [], 0
        for nr in live_rows:
            intra.append(jnp.sum(a_rep[off:off + nr, :] * ii[:nr], axis=0, keepdims=True))
            off += nr
        o = inter + jnp.concatenate(intra, axis=0)
        b_last = b[C - 1:C, :]
        kdec = kk * jnp.exp(b_last - b)
        st_scr[h] = st * jnp.exp(b_last) + _dot_tn(ii.astype(BF16), kdec.astype(BF16))
        o_ref[rows, cols] = _rms(o, nw) * jax.nn.silu(hg_ref[rows, cols])

    def body(c, carry):
        r0 = pl.multiple_of(c * C, C)
        rows = pl.ds(r0, C)
        for h in range(HGRN_HEADS):
            one_head(h, r0, rows)
        return carry

    lax.fori_loop(0, ts // C, body, 0, unroll=min(2, ts // C))

    @pl.when(stretch == pl.num_programs(1) - 1)
    def _():
        for h in range(HGRN_HEADS):
            sfin_ref[0, h] = st_scr[h].T


def _hgrn(g4, lb_logits, norm_w, state, batch, seq, ts, n_valid, layer):
    has_state = state is not None
    nh = HGRN_HEADS
    per_seq = seq // ts
    col = lambda part: pl.BlockSpec((ts, HGRN_WIDTH), lambda b, s, part=part: (b * per_seq + s, part))
    st_spec = pl.BlockSpec((1, nh, HGRN_DK, HGRN_DK), lambda b, s: (b, 0, 0, 0))
    in_specs = [col(0), col(1), col(2), col(3),
                pl.BlockSpec(lb_logits.shape, lambda b, s: (0, 0)),
                pl.BlockSpec((1, HGRN_DK), lambda b, s: (0, 0))]
    args = [g4, g4, g4, g4, lb_logits, norm_w]
    if has_state:
        in_specs.append(st_spec)
        args.append(state)
    return pl.pallas_call(
        functools.partial(_hgrn_kernel, ts=ts, n_valid=n_valid, has_state=has_state, layer=layer),
        grid=(batch, per_seq),
        in_specs=in_specs,
        out_specs=[pl.BlockSpec((ts, HGRN_WIDTH), lambda b, s: (b * per_seq + s, 0)), st_spec],
        out_shape=[jax.ShapeDtypeStruct((batch * seq, HGRN_WIDTH), F32),
                   jax.ShapeDtypeStruct((batch, nh, HGRN_DK, HGRN_DK), F32)],
        scratch_shapes=[pltpu.VMEM((nh, HGRN_DK, HGRN_DK), F32)],
        compiler_params=_params("parallel", "arbitrary"),
        name="hgrn",
    )(*args)


def _out_proj_kernel(x_ref, att_ref, og_ref, wo_ref, nw_ref, wq_ref, h_ref, xn_ref, qp_ref):
    h = (x_ref[...]
         + _dot(att_ref[...].astype(BF16), wo_ref[:ATT_WIDTH, :])
         + _dot(og_ref[...].astype(BF16), wo_ref[ATT_WIDTH:, :]))
    h_ref[...] = h
    xn = _rms(h, nw_ref[...]).astype(BF16)
    xn_ref[...] = xn
    qp_ref[...] = _dot(xn, wq_ref[...])


def _out_proj(x, att, og, w_out_b, norm_w, wq_b, tm):
    n = x.shape[0]
    qcols = wq_b.shape[1]
    row = lambda i: (i, 0)
    fixed = lambda i: (0, 0)
    return pl.pallas_call(
        _out_proj_kernel,
        grid=(n // tm,),
        in_specs=[pl.BlockSpec((tm, D_MODEL), row),
                  pl.BlockSpec((tm, ATT_WIDTH), row),
                  pl.BlockSpec((tm, HGRN_WIDTH), row),
                  pl.BlockSpec((ATT_WIDTH + HGRN_WIDTH, D_MODEL), fixed),
                  pl.BlockSpec((1, D_MODEL), fixed),
                  pl.BlockSpec((D_MODEL, qcols), fixed)],
        out_specs=[pl.BlockSpec((tm, D_MODEL), row),
                   pl.BlockSpec((tm, D_MODEL), row),
                   pl.BlockSpec((tm, qcols), row)],
        out_shape=[jax.ShapeDtypeStruct((n, D_MODEL), F32),
                   jax.ShapeDtypeStruct((n, D_MODEL), BF16),
                   jax.ShapeDtypeStruct((n, qcols), F32)],
        compiler_params=_params("parallel"),
        name="out_proj",
    )(x, att, og, w_out_b, norm_w, wq_b)


def _top_values(s, count):
    rows = lax.broadcasted_iota(jnp.int32, (count, s.shape[1]), 0)
    out = jnp.zeros((count, s.shape[1]), F32)
    cur = s
    for r in range(count):
        m = jnp.max(cur, axis=0, keepdims=True)
        out = jnp.where(rows == r, m, out)
        if r + 1 < count:
            cur = jnp.where(cur == m, NEG_INF, cur)
    return out


def _peer_scores_kernel(qp_ref, keys_ref, s2_ref, th_ref, e1_ref, e2_ref, th_scr, e1_scr):
    tn = qp_ref.shape[0]
    k = PEER_TOPK
    half = PEER_DKEY // 2
    rowk = lax.broadcasted_iota(jnp.int32, (k, tn), 0)
    for h in range(PEER_HEADS):
        q1 = qp_ref[:, h * PEER_DKEY:h * PEER_DKEY + half].astype(BF16)
        q2 = qp_ref[:, h * PEER_DKEY + half:(h + 1) * PEER_DKEY].astype(BF16)
        s1 = _dot_nt(keys_ref[h, 0], q1)
        s2 = _dot_nt(keys_ref[h, 1], q2)
        t1 = _top_values(s1, k)
        t2 = _top_values(s2, k)
        cands = []
        for i in range(k // 2):
            cands.append(jnp.where(rowk < k // (i + 1), t1[i:i + 1, :] + t2, NEG_INF))
        cands.append(t1[k // 2:, :] + t2[0:1, :])
        cand = jnp.concatenate(cands, axis=0)
        top = t1[0:1, :] + t2[0:1, :]
        cur = cand
        for _ in range(k - 1):
            cur = jnp.where(cur == jnp.max(cur, axis=0, keepdims=True), NEG_INF, cur)
        tau = jnp.max(cur, axis=0, keepdims=True)
        z = jnp.sum(jnp.where(cand >= tau, jnp.exp(cand - top), 0.0), axis=0, keepdims=True)
        best2 = t2[0:1, :]
        th = jnp.where(s1 + best2 >= tau, best2, jnp.inf)
        for i in range(k // 2):
            t1i = t1[i:i + 1, :]
            thr = jnp.full((1, tn), jnp.inf, F32)
            for j in range(k // (i + 1)):
                t2j = t2[j:j + 1, :]
                thr = jnp.where(t1i + t2j >= tau, t2j, thr)
            th = jnp.where(s1 == t1i, thr, th)
        s2_ref[h] = s2
        th_scr[h] = th
        e1_scr[h] = jnp.exp(s1 - t1[0:1, :])
        e2_ref[h] = jnp.exp(s2 - t2[0:1, :]) * (0.5 / z)
    th_ref[...] = jnp.transpose(th_scr[...], (1, 0, 2))
    e1_ref[...] = jnp.transpose(e1_scr[...], (1, 0, 2))


def _peer_scores(qp, keys_b, tn):
    n = qp.shape[0]
    by_head = pl.BlockSpec((PEER_HEADS, N_KEYS, tn), lambda i: (0, 0, i))
    by_key = pl.BlockSpec((N_KEYS, PEER_HEADS, tn), lambda i: (0, 0, i))
    head_shape = jax.ShapeDtypeStruct((PEER_HEADS, N_KEYS, n), F32)
    key_shape = jax.ShapeDtypeStruct((N_KEYS, PEER_HEADS, n), F32)
    return pl.pallas_call(
        _peer_scores_kernel,
        grid=(n // tn,),
        in_specs=[pl.BlockSpec((tn, PEER_HEADS * PEER_DKEY), lambda i: (i, 0)),
                  pl.BlockSpec((PEER_HEADS, 2, N_KEYS, PEER_DKEY // 2), lambda i: (0, 0, 0, 0))],
        out_specs=[by_head, by_key, by_key, by_head],
        out_shape=[head_shape, key_shape, key_shape, head_shape],
        scratch_shapes=[pltpu.VMEM((PEER_HEADS, N_KEYS, tn), F32)] * 2,
        compiler_params=_params("parallel"),
        name="peer_scores",
    )(qp, keys_b)


def _peer_kernel(x_ref, h_ref, u_ref, vt_ref, s2_ref, th_ref, e1_ref, e2_ref, nfw_ref,
                 o_ref, acc_scr, z_scr, w_scr, c0_scr, c1_scr, *, eb, tn, nblk):
    j = pl.program_id(1)
    slabs = eb // N_KEYS

    def coefficients(c_ref):
        z_scr[...] = _dot_nt(u_ref[...], x_ref[...])
        zeros = [jnp.zeros((SUBLANES, LANES), F32)] * 2
        tile = 0
        for al in range(slabs):
            a = j * slabs + al
            for tc in range(tn // LANES):
                lanes = slice(tc * LANES, (tc + 1) * LANES)
                ths = [th_ref[a, h:h + 1, lanes] for h in range(PEER_HEADS)]
                e1s = [e1_ref[a, h:h + 1, lanes] for h in range(PEER_HEADS)]
                zero = zeros[tile % 2]
                for v in range(N_KEYS // SUBLANES):
                    brow = slice(v * SUBLANES, (v + 1) * SUBLANES)
                    w = zero
                    for h in range(PEER_HEADS):
                        w = w + jnp.where(s2_ref[h, brow, lanes] >= ths[h],
                                          e2_ref[h, brow, lanes], 0.0) * e1s[h]
                    if v == 0:
                        zeros[tile % 2] = pltpu.roll(pltpu.roll(w * 0.0, 1, axis=1), 1, axis=1)
                    w_scr[al * N_KEYS + v * SUBLANES:al * N_KEYS + (v + 1) * SUBLANES, lanes] = w
                tile += 1
        rows = 2 * SUBLANES
        for r in range(eb // rows):
            erow = slice(r * rows, (r + 1) * rows)
            z = z_scr[erow, :]
            c_ref[erow, :] = (w_scr[erow, :] * (z * (1.0 + lax.erf(z * math.sqrt(0.5))))).astype(BF16)

    def accumulate(c_ref):
        acc_scr[...] += _dot(vt_ref[...], c_ref[...])

    odd = (j % 2) == 1
    inner = jnp.logical_and(j > 0, j < nblk)

    @pl.when(j == 0)
    def _():
        acc_scr[...] = jnp.zeros_like(acc_scr)
        coefficients(c0_scr)

    @pl.when(jnp.logical_and(inner, odd))
    def _():
        accumulate(c0_scr)
        coefficients(c1_scr)

    @pl.when(jnp.logical_and(inner, jnp.logical_not(odd)))
    def _():
        accumulate(c1_scr)
        coefficients(c0_scr)

    @pl.when(j == nblk)
    def _():
        accumulate(c1_scr if (nblk - 1) % 2 == 1 else c0_scr)
        hh = h_ref[...] + acc_scr[...].T
        o_ref[...] = _rms(hh, nfw_ref[...])


def _peer(xn, h, u_b, vt_b, s2, th, e1, e2, nf_w, tn, eb):
    n = xn.shape[0]
    nblk = u_b.shape[0] // eb
    tok = pl.BlockSpec((tn, D_MODEL), lambda i, j: (i, 0))
    by_head = pl.BlockSpec((PEER_HEADS, N_KEYS, tn), lambda i, j: (0, 0, i))
    by_key = pl.BlockSpec((N_KEYS, PEER_HEADS, tn), lambda i, j: (0, 0, i))
    return pl.pallas_call(
        functools.partial(_peer_kernel, eb=eb, tn=tn, nblk=nblk),
        grid=(n // tn, nblk + 1),
        in_specs=[tok, tok,
                  pl.BlockSpec((eb, D_MODEL), lambda i, j: (jnp.minimum(j, nblk - 1), 0)),
                  pl.BlockSpec((D_MODEL, eb), lambda i, j: (0, jnp.maximum(j - 1, 0))),
                  by_head, by_key, by_key, by_head,
                  pl.BlockSpec((1, D_MODEL), lambda i, j: (0, 0))],
        out_specs=tok,
        out_shape=jax.ShapeDtypeStruct((n, D_MODEL), F32),
        scratch_shapes=[pltpu.VMEM((D_MODEL, tn), F32),
                        pltpu.VMEM((eb, tn), F32),
                        pltpu.VMEM((eb, tn), F32),
                        pltpu.VMEM((eb, tn), BF16),
                        pltpu.VMEM((eb, tn), BF16)],
        compiler_params=_params("parallel", "arbitrary"),
        name="peer_experts",
    )(xn, h, u_b, vt_b, s2, th, e1, e2, nf_w)


def _tail(x, att, og, w, tm, tn, eb):
    h, xn, qp = _out_proj(x, att, og, w["w_out"], w["norm2"], w["wq"], tm)
    s2, th, e1, e2 = _peer_scores(qp, w["keys"], min(tn, 256))
    return _peer(xn, h, w["u"], w["vt"], s2, th, e1, e2, w["norm_f"], tn, eb)


def _window_view(t):
    return jnp.transpose(t, (0, 2, 3, 1))


def _window_unview(t):
    return jnp.transpose(t, (0, 3, 1, 2))


def kernel(x_prompt, x_sample, cache_k_win, cache_v_win, state_hgrn, norm1_w, w_in,
           hgrn_norm_w, hgrn_lb_logits, w_out, norm2_w, peer_wq, peer_sub_keys,
           peer_u, peer_v, norm_f_w):
    layer = 0
    batch, seq, _ = x_prompt.shape
    dec_batch, dec_seq, _ = x_sample.shape
    w_in_b = w_in[layer].astype(BF16)
    w = {
        "norm1": norm1_w[layer][None, :],
        "w_in": w_in_b,
        "w_kvt": w_in_b[:, ATT_WIDTH:3 * ATT_WIDTH].T,
        "hgrn_norm": hgrn_norm_w[layer][None, :],
        "w_out": w_out[layer].astype(BF16),
        "norm2": norm2_w[layer][None, :],
        "wq": peer_wq[layer].astype(BF16),
        "keys": peer_sub_keys[layer].astype(BF16),
        "u": peer_u[layer].astype(BF16),
        "vt": peer_v[layer].astype(BF16).T,
        "norm_f": norm_f_w[None, :],
    }
    heads = (ATT_HEADS, HEAD_DIM)

    xp = x_prompt.reshape(batch * seq, D_MODEL)
    q, k, v, g4, kt, vt = _in_proj(xp, w["norm1"], w["w_in"], w["w_kvt"], 512, seq)
    att = _attn_prompt(q, k, v, batch, seq)
    og, s_prompt = _hgrn(g4, hgrn_lb_logits, w["hgrn_norm"], None, batch, seq, seq // 2, None, layer)
    y_prompt = _tail(xp, att, og, w, 512, 512, PEER_EXPERT_BLOCK).reshape(batch, seq, D_MODEL)
    keep = min(max(wd for wd, _ in ATT_BRANCHES), seq)
    k_win_prompt = _window_unview(kt.reshape(batch, *heads, seq)[..., seq - keep:])
    v_win_prompt = _window_unview(vt.reshape(batch, *heads, seq)[..., seq - keep:])

    tms = dec_batch * dec_seq
    xs = x_sample.reshape(tms, D_MODEL)
    qs, _, _, g4s, kts, vts = _in_proj(xs, w["norm1"], w["w_in"], w["w_kvt"], tms, tms)
    rows = SUBLANES
    q_pad = jnp.pad(jnp.transpose(qs.reshape(dec_batch, dec_seq, *heads), (0, 2, 1, 3)),
                    ((0, 0), (0, 0), (0, rows - dec_seq), (0, 0)))
    new_t = lambda t: jnp.pad(
        jnp.transpose(t.reshape(*heads, dec_batch, dec_seq), (2, 0, 1, 3)),
        ((0, 0), (0, 0), (0, 0), (LANES - dec_seq, 0)))
    att_s, k_buf, v_buf = _attn_sample(q_pad, new_t(kts), new_t(vts),
                                       _window_view(cache_k_win[layer]),
                                       _window_view(cache_v_win[layer]), dec_seq)
    att_s = jnp.transpose(att_s[:, :, :dec_seq], (0, 2, 1, 3)).reshape(tms, ATT_WIDTH)
    seq_pad = HGRN_CHUNK
    g4p = jnp.pad(g4s.reshape(dec_batch, dec_seq, 4 * HGRN_WIDTH),
                  ((0, 0), (0, seq_pad - dec_seq), (0, 0))).reshape(dec_batch * seq_pad, 4 * HGRN_WIDTH)
    ogp, s_sample = _hgrn(g4p, hgrn_lb_logits, w["hgrn_norm"], state_hgrn[layer],
                          dec_batch, seq_pad, seq_pad, dec_seq, layer)
    og_s = ogp.reshape(dec_batch, seq_pad, HGRN_WIDTH)[:, :dec_seq].reshape(tms, HGRN_WIDTH)
    y_sample = _tail(xs, att_s, og_s, w, tms, tms, PEER_EXPERT_BLOCK).reshape(dec_batch, dec_seq, D_MODEL)

    return (y_prompt, y_sample,
            k_win_prompt[None], v_win_prompt[None],
            s_prompt[None],
            _window_unview(k_buf)[None], _window_unview(v_buf)[None],
            s_sample.astype(state_hgrn.dtype)[None])
```

```python
import functools
import math

import jax
import jax.numpy as jnp
from jax import lax
from jax.experimental import pallas as pl
from jax.experimental.pallas import tpu as pltpu

F32 = jnp.float32
BF16 = jnp.bfloat16

D_MODEL = 1024
HEAD_DIM = 64
ATT_WIDTH = 512
ATT_HEADS = 8
ATT_BRANCHES = ((128, 1), (512, 4), (2048, 16))
ATT_STEPS = 128
ATT_SCALE = HEAD_DIM ** -0.5
ATT_UNROLL = 8
ATT_SAMPLE_HEADS = 4
HGRN_WIDTH = 512
HGRN_DK = 128
HGRN_HEADS = 4
HGRN_CHUNK = 16
IN_COLS = 3 * ATT_WIDTH + 4 * HGRN_WIDTH
N_KEYS = 128
PEER_HEADS = 8
PEER_DKEY = 256
PEER_TOPK = 16
EPS = 1e-6
NEG_INF = float("-inf")

LANES = 128
SUBLANES = 8
PEER_EXPERT_BLOCK = SUBLANES * N_KEYS
VMEM_LIMIT = 56 * 1024 * 1024


def _dot(a, b):
    return jnp.dot(a, b, preferred_element_type=F32)


def _dot_nt(a, b):
    return lax.dot_general(a, b, (((1,), (1,)), ((), ())), preferred_element_type=F32)


def _dot_tn(a, b):
    return lax.dot_general(a, b, (((0,), (0,)), ((), ())), preferred_element_type=F32)


def _rms(x, w):
    return x * lax.rsqrt(jnp.mean(x * x, axis=-1, keepdims=True) + EPS) * w


def _params(*sem):
    return pltpu.CompilerParams(dimension_semantics=sem, vmem_limit_bytes=VMEM_LIMIT)


def _in_proj_kernel(x_ref, nw_ref, w_ref, wkv_ref, q_ref, k_ref, v_ref, g_ref, kt_ref, vt_ref):
    xn = _rms(x_ref[...], nw_ref[...]).astype(BF16)
    p = _dot(xn, w_ref[...])
    q_ref[...] = p[:, :ATT_WIDTH] * ATT_SCALE
    k_ref[...] = p[:, ATT_WIDTH:2 * ATT_WIDTH]
    v_ref[...] = p[:, 2 * ATT_WIDTH:3 * ATT_WIDTH]
    g_ref[...] = p[:, 3 * ATT_WIDTH:]
    kvt = _dot_nt(wkv_ref[...], xn)
    kt_ref[0] = kvt[:ATT_WIDTH]
    vt_ref[0] = kvt[ATT_WIDTH:]


def _in_proj(x, norm_w, w_in_b, w_kvt_b, tm, seq):
    n = x.shape[0]
    per_seq = seq // tm
    row = lambda i: (i, 0)
    fixed = lambda i: (0, 0)
    tr = lambda i: (i // per_seq, 0, i % per_seq)
    return pl.pallas_call(
        _in_proj_kernel,
        grid=(n // tm,),
        in_specs=[pl.BlockSpec((tm, D_MODEL), row),
                  pl.BlockSpec((1, D_MODEL), fixed),
                  pl.BlockSpec((D_MODEL, IN_COLS), fixed),
                  pl.BlockSpec((2 * ATT_WIDTH, D_MODEL), fixed)],
        out_specs=[pl.BlockSpec((tm, ATT_WIDTH), row),
                   pl.BlockSpec((tm, ATT_WIDTH), row),
                   pl.BlockSpec((tm, ATT_WIDTH), row),
                   pl.BlockSpec((tm, 4 * HGRN_WIDTH), row),
                   pl.BlockSpec((1, ATT_WIDTH, tm), tr),
                   pl.BlockSpec((1, ATT_WIDTH, tm), tr)],
        out_shape=[jax.ShapeDtypeStruct((n, ATT_WIDTH), F32),
                   jax.ShapeDtypeStruct((n, ATT_WIDTH), F32),
                   jax.ShapeDtypeStruct((n, ATT_WIDTH), F32),
                   jax.ShapeDtypeStruct((n, 4 * HGRN_WIDTH), F32),
                   jax.ShapeDtypeStruct((n // seq, ATT_WIDTH, seq), F32),
                   jax.ShapeDtypeStruct((n // seq, ATT_WIDTH, seq), F32)],
        compiler_params=_params("parallel"),
        name="in_proj",
    )(x, norm_w, w_in_b, w_kvt_b)


def _attn_prompt_kernel(q_ref, k_ref, v_ref, o_ref, qc_scr, kc_scr, vc_scr, m_scr, l_scr, a_scr,
                        *, seq):
    blk = ATT_STEPS
    (_, d0), (_, d1), (_, d2) = ATT_BRANCHES
    assert d0 == 1 and d2 % d1 == 0
    cls = seq // d1
    sub = d2 // d1
    lo = lax.broadcasted_iota(jnp.int32, (blk, LANES), 1) < HEAD_DIM

    for r in range(d1):
        src = pl.ds(r, cls, stride=d1)
        dst = pl.ds(r * cls, cls)
        qc_scr[dst, :] = q_ref[src, :]
        kc_scr[dst, :] = k_ref[src, :]
        vc_scr[dst, :] = v_ref[src, :]

    def attend(srcs, br, stride, q_start, k_start, n_keys, off):
        q_src, k_src, v_src = srcs
        step = {} if stride == 1 else {"stride": stride}
        qrows = pl.ds(q_start, blk, **step)
        krows = pl.ds(k_start, n_keys, **step)
        qb = q_src[qrows, :]
        kb = k_src[krows, :].astype(BF16)
        vb = v_src[krows, :]
        dist = (off + lax.broadcasted_iota(jnp.int32, (blk, n_keys), 0)
                - lax.broadcasted_iota(jnp.int32, (blk, n_keys), 1))
        ok = lax.bitcast_convert_type(dist, jnp.uint32) <= blk
        lo_k = lax.broadcasted_iota(jnp.int32, (n_keys, LANES), 1) < HEAD_DIM
        ms, ps, vhs = [], [], []
        for first_head in (True, False):
            qh = (jnp.where(lo, qb, 0.0) if first_head else jnp.where(lo, 0.0, qb)).astype(BF16)
            s = jnp.where(ok, _dot_nt(qh, kb), NEG_INF)
            m = jnp.max(s, axis=-1, keepdims=True)
            ms.append(m)
            ps.append(jnp.exp(s - m).astype(BF16))
            vhs.append((jnp.where(lo_k, vb, 1.0) if first_head else jnp.where(lo_k, 1.0, vb)).astype(BF16))
        return qrows, ms, ps, vhs

    def finish(br, qrows, ms, ps, vhs):
        rs = [_dot(p, vh) for p, vh in zip(ps, vhs)]
        m_scr[br, qrows, :] = jnp.where(lo, ms[0], ms[1])
        a_scr[br, qrows, :] = jnp.where(lo, rs[0], rs[1])
        l_scr[br, qrows, :] = jnp.where(lo, rs[1], rs[0])

    def run_branch(srcs, br, stride, n_classes, class_base, class_len):
        nb = class_len // blk

        def one(idx):
            n = idx // n_classes
            c = idx - n * n_classes
            base = class_base(c)
            if nb == 1:
                return attend(srcs, br, stride, base, base, blk, 0)
            first = n == 0
            q_start = base + n * (blk * stride)
            k_start = base + jnp.where(first, 0, n - 1) * (blk * stride)
            return attend(srcs, br, stride, q_start, k_start, 2 * blk, jnp.where(first, 0, blk))

        def body(trip, carry):
            held = [one(trip * ATT_UNROLL + g) for g in range(ATT_UNROLL)]
            for parts in held:
                finish(br, *parts)
            return carry

        lax.fori_loop(0, nb * n_classes // ATT_UNROLL, body, 0)

    natural = (q_ref, k_ref, v_ref)
    by_class = (qc_scr, kc_scr, vc_scr)
    run_branch(natural, 0, 1, 1, lambda c: 0, seq)
    run_branch(by_class, 1, 1, d1, lambda c: c * cls, cls)
    run_branch(by_class, 2, sub, d2, lambda c: (c // sub) * cls + c % sub, cls // sub)

    chunk = 256
    per_class = cls // chunk

    def combine(idx, carry):
        r = idx // per_class
        c = idx - r * per_class
        nat = pl.ds(r + c * (chunk * d1), chunk, stride=d1)
        byc = pl.ds(pl.multiple_of(r * cls + c * chunk, chunk), chunk)
        rows = (nat, byc, byc)
        m = [m_scr[b, rows[b], :] for b in range(3)]
        mx = jnp.maximum(jnp.maximum(m[0], m[1]), m[2])
        den = jnp.zeros((chunk, LANES), F32)
        num = jnp.zeros((chunk, LANES), F32)
        for b in range(3):
            w = jnp.exp(m[b] - mx)
            den = den + w * pltpu.roll(l_scr[b, rows[b], :], HEAD_DIM, axis=1)
            num = num + w * a_scr[b, rows[b], :]
        o_ref[nat, :] = num / den
        return carry

    lax.fori_loop(0, d1 * per_class, combine, 0)


def _attn_prompt(q, k, v, batch, seq):
    spec = pl.BlockSpec((seq, LANES), lambda b, hp: (b, hp))
    return pl.pallas_call(
        functools.partial(_attn_prompt_kernel, seq=seq),
        grid=(batch, ATT_WIDTH // LANES),
        in_specs=[spec, spec, spec],
        out_specs=spec,
        out_shape=jax.ShapeDtypeStruct((batch * seq, ATT_WIDTH), F32),
        scratch_shapes=[pltpu.VMEM((seq, LANES), F32)] * 3 + [pltpu.VMEM((3, seq, LANES), F32)] * 3,
        compiler_params=_params("parallel", "parallel"),
        name="attn_prompt",
    )(q, k, v)


def _attn_sample_kernel(q_ref, kn_ref, vn_ref, kc_ref, vc_ref, o_ref, ko_ref, vo_ref,
                        *, n_new, n_buf):
    for g in range(q_ref.shape[1]):
        _attn_sample_head(q_ref.at[0, g], kn_ref.at[0, g], vn_ref.at[0, g], kc_ref.at[0, g],
                          vc_ref.at[0, g], o_ref.at[0, g], ko_ref.at[0, g], vo_ref.at[0, g],
                          n_new=n_new, n_buf=n_buf)


def _attn_sample_head(q_ref, kn_ref, vn_ref, kc_ref, vc_ref, o_ref, ko_ref, vo_ref, *, n_new, n_buf):
    rows = q_ref.shape[0]
    q = q_ref[...].astype(BF16)
    kc = kc_ref[...]
    vc = vc_ref[...]
    kn = kn_ref[...]
    vn = vn_ref[...]
    s_c = _dot(q, kc.astype(BF16))
    s_n = _dot(q, kn.astype(BF16))
    tok_c = lax.broadcasted_iota(jnp.int32, (rows, n_buf), 0)
    dist_c = n_buf + tok_c - lax.broadcasted_iota(jnp.int32, (rows, n_buf), 1)
    tok_n = lax.broadcasted_iota(jnp.int32, (rows, LANES), 0)
    key_n = lax.broadcasted_iota(jnp.int32, (rows, LANES), 1) - (LANES - n_new)
    dist_n = tok_n - key_n
    parts = []
    for window, dil in ATT_BRANCHES:
        ok_c = jnp.where((dist_c & (dil - 1)) == 0, dist_c, window + 1) <= window
        ok_n = lax.bitcast_convert_type(
            jnp.where((dist_n & (dil - 1)) == 0, jnp.where(key_n >= 0, dist_n, -1), -1),
            jnp.uint32) <= window
        sc = jnp.where(ok_c, s_c, NEG_INF)
        sn = jnp.where(ok_n, s_n, NEG_INF)
        m = jnp.maximum(jnp.max(sc, axis=-1, keepdims=True), jnp.max(sn, axis=-1, keepdims=True))
        pc = jnp.exp(sc - m)
        pn = jnp.exp(sn - m)
        l = jnp.sum(pc, axis=-1, keepdims=True) + jnp.sum(pn, axis=-1, keepdims=True)
        acc = (_dot_nt(pc.astype(BF16), vc.astype(BF16))
               + _dot_nt(pn.astype(BF16), vn.astype(BF16)))
        parts.append((m, l, acc))
    mx = jnp.maximum(jnp.maximum(parts[0][0], parts[1][0]), parts[2][0])
    den = jnp.zeros((rows, 1), F32)
    num = jnp.zeros((rows, HEAD_DIM), F32)
    for m, l, acc in parts:
        w = jnp.exp(m - mx)
        den = den + w * l
        num = num + w * acc
    o_ref[...] = num / den
    fresh = lax.broadcasted_iota(jnp.int32, (HEAD_DIM, LANES), 1) >= LANES - n_new
    for c_val, n_val, out_ref in ((kc, kn, ko_ref), (vc, vn, vo_ref)):
        shifted = pltpu.roll(c_val, n_buf - n_new, axis=1)
        out_ref[:, :n_buf - LANES] = shifted[:, :n_buf - LANES]
        out_ref[:, n_buf - LANES:] = jnp.where(fresh, n_val, shifted[:, n_buf - LANES:])


def _attn_sample(q_pad, kn_t, vn_t, cache_kt, cache_vt, n_new):
    batch, heads, rows, _ = q_pad.shape
    n_buf = cache_kt.shape[3]
    idx = lambda b, h: (b, h, 0, 0)
    group = ATT_SAMPLE_HEADS
    buf_spec = pl.BlockSpec((1, group, HEAD_DIM, n_buf), idx)
    new_spec = pl.BlockSpec((1, group, HEAD_DIM, LANES), idx)
    q_spec = pl.BlockSpec((1, group, rows, HEAD_DIM), idx)
    buf_shape = jax.ShapeDtypeStruct((batch, heads, HEAD_DIM, n_buf), F32)
    return pl.pallas_call(
        functools.partial(_attn_sample_kernel, n_new=n_new, n_buf=n_buf),
        grid=(batch, heads // group),
        in_specs=[q_spec, new_spec, new_spec, buf_spec, buf_spec],
        out_specs=[q_spec, buf_spec, buf_spec],
        out_shape=[jax.ShapeDtypeStruct((batch, heads, rows, HEAD_DIM), F32), buf_shape, buf_shape],
        compiler_params=_params("parallel", "parallel"),
        name="attn_sample",
    )(q_pad, kn_t, vn_t, cache_kt, cache_vt)


def _hgrn_kernel(*refs, ts, n_valid, has_state, layer):
    if has_state:
        hq_ref, hf_ref, hi_ref, hg_ref, lbl_ref, nw_ref, s0_ref, o_ref, sfin_ref, st_scr = refs
    else:
        hq_ref, hf_ref, hi_ref, hg_ref, lbl_ref, nw_ref, o_ref, sfin_ref, st_scr = refs
    C = HGRN_CHUNK
    stretch = pl.program_id(1)

    @pl.when(stretch == 0)
    def _():
        for h in range(HGRN_HEADS):
            if has_state:
                st_scr[h] = s0_ref[0, h].T
            else:
                st_scr[h] = jnp.zeros((HGRN_DK, HGRN_DK), F32)

    lg = lbl_ref[...]
    ex = jnp.exp(lg - jnp.max(lg, axis=0, keepdims=True))
    lb_all = jnp.sum(ex[:layer + 1], axis=0, keepdims=True) / jnp.sum(ex, axis=0, keepdims=True)
    nw = nw_ref[...]
    rowi = lax.broadcasted_iota(jnp.int32, (C, HGRN_DK), 0)
    ones = jnp.ones((HGRN_DK, LANES), BF16)
    base = stretch * ts

    def one_head(h, r0, rows):
        cols = slice(h * HGRN_DK, (h + 1) * HGRN_DK)
        lb = lb_all[:, cols]
        f = lb + (1.0 - lb) * jax.nn.sigmoid(hf_ref[rows, cols])
        logf = jnp.log(f)
        kk = 1.0 - f
        if n_valid is not None:
            live = (rowi + (base + r0)) < n_valid
            logf = jnp.where(live, logf, 0.0)
            kk = jnp.where(live, kk, 0.0)
        q = jax.nn.silu(hq_ref[rows, cols])
        ii = hi_ref[rows, cols]
        b = logf
        sh = 1
        while sh < C:
            b = b + jnp.where(rowi >= sh, pltpu.roll(b, sh, axis=0), 0.0)
            sh *= 2
        st = st_scr[h]
        inter = _dot_nt((q * jnp.exp(b)).astype(BF16), st.astype(BF16))
        live_rows = [min(C, SUBLANES * (t // SUBLANES + 1)) for t in range(C)]
        prods = []
        for t, nr in enumerate(live_rows):
            e = jnp.exp(jnp.where(rowi[:nr] <= t, b[t:t + 1, :] - b[:nr], NEG_INF))
            prods.append(q[t:t + 1, :] * e * kk[:nr])
        a_rep = _dot(jnp.concatenate(prods, axis=0).astype(BF16), ones)
        intra, off = [], 0
        for nr in live_rows:
            intra.append(jnp.sum(a_rep[off:off + nr, :] * ii[:nr], axis=0, keepdims=True))
            off += nr
        o = inter + jnp.concatenate(intra, axis=0)
        b_last = b[C - 1:C, :]
        kdec = kk * jnp.exp(b_last - b)
        st_scr[h] = st * jnp.exp(b_last) + _dot_tn(ii.astype(BF16), kdec.astype(BF16))
        o_ref[rows, cols] = _rms(o, nw) * jax.nn.silu(hg_ref[rows, cols])

    def body(c, carry):
        r0 = pl.multiple_of(c * C, C)
        rows = pl.ds(r0, C)
        for h in range(HGRN_HEADS):
            one_head(h, r0, rows)
        return carry

    lax.fori_loop(0, ts // C, body, 0, unroll=min(4, ts // C))

    @pl.when(stretch == pl.num_programs(1) - 1)
    def _():
        for h in range(HGRN_HEADS):
            sfin_ref[0, h] = st_scr[h].T


def _hgrn(g4, lb_logits, norm_w, state, batch, seq, ts, n_valid, layer):
    has_state = state is not None
    nh = HGRN_HEADS
    per_seq = seq // ts
    col = lambda part: pl.BlockSpec((ts, HGRN_WIDTH), lambda b, s, part=part: (b * per_seq + s, part))
    st_spec = pl.BlockSpec((1, nh, HGRN_DK, HGRN_DK), lambda b, s: (b, 0, 0, 0))
    in_specs = [col(0), col(1), col(2), col(3),
                pl.BlockSpec(lb_logits.shape, lambda b, s: (0, 0)),
                pl.BlockSpec((1, HGRN_DK), lambda b, s: (0, 0))]
    args = [g4, g4, g4, g4, lb_logits, norm_w]
    if has_state:
        in_specs.append(st_spec)
        args.append(state)
    return pl.pallas_call(
        functools.partial(_hgrn_kernel, ts=ts, n_valid=n_valid, has_state=has_state, layer=layer),
        grid=(batch, per_seq),
        in_specs=in_specs,
        out_specs=[pl.BlockSpec((ts, HGRN_WIDTH), lambda b, s: (b * per_seq + s, 0)), st_spec],
        out_shape=[jax.ShapeDtypeStruct((batch * seq, HGRN_WIDTH), F32),
                   jax.ShapeDtypeStruct((batch, nh, HGRN_DK, HGRN_DK), F32)],
        scratch_shapes=[pltpu.VMEM((nh, HGRN_DK, HGRN_DK), F32)],
        compiler_params=_params("parallel", "arbitrary"),
        name="hgrn",
    )(*args)


def _out_proj_kernel(x_ref, att_ref, og_ref, wo_ref, nw_ref, wq_ref, h_ref, xn_ref, qp_ref):
    h = (x_ref[...]
         + _dot(att_ref[...].astype(BF16), wo_ref[:ATT_WIDTH, :])
         + _dot(og_ref[...].astype(BF16), wo_ref[ATT_WIDTH:, :]))
    h_ref[...] = h
    xn = _rms(h, nw_ref[...]).astype(BF16)
    xn_ref[...] = xn
    qp_ref[...] = _dot(xn, wq_ref[...])


def _out_proj(x, att, og, w_out_b, norm_w, wq_b, tm):
    n = x.shape[0]
    qcols = wq_b.shape[1]
    row = lambda i: (i, 0)
    fixed = lambda i: (0, 0)
    return pl.pallas_call(
        _out_proj_kernel,
        grid=(n // tm,),
        in_specs=[pl.BlockSpec((tm, D_MODEL), row),
                  pl.BlockSpec((tm, ATT_WIDTH), row),
                  pl.BlockSpec((tm, HGRN_WIDTH), row),
                  pl.BlockSpec((ATT_WIDTH + HGRN_WIDTH, D_MODEL), fixed),
                  pl.BlockSpec((1, D_MODEL), fixed),
                  pl.BlockSpec((D_MODEL, qcols), fixed)],
        out_specs=[pl.BlockSpec((tm, D_MODEL), row),
                   pl.BlockSpec((tm, D_MODEL), row),
                   pl.BlockSpec((tm, qcols), row)],
        out_shape=[jax.ShapeDtypeStruct((n, D_MODEL), F32),
                   jax.ShapeDtypeStruct((n, D_MODEL), BF16),
                   jax.ShapeDtypeStruct((n, qcols), F32)],
        compiler_params=_params("parallel"),
        name="out_proj",
    )(x, att, og, w_out_b, norm_w, wq_b)


def _bitonic_merge(v):
    n = len(v)
    j = n // 2
    while j >= 1:
        for i in range(n):
            l = i ^ j
            if l > i:
                v[i], v[l] = jnp.maximum(v[i], v[l]), jnp.minimum(v[i], v[l])
        j //= 2
    return v


def _top_values(s, count):
    assert s.shape[0] == count * SUBLANES and count & (count - 1) == 0
    v = [s[i * SUBLANES:(i + 1) * SUBLANES] for i in range(count)]
    k = 2
    while k <= count:
        j = k // 2
        while j >= 1:
            for i in range(count):
                l = i ^ j
                if l > i:
                    hi, lo = jnp.maximum(v[i], v[l]), jnp.minimum(v[i], v[l])
                    v[i], v[l] = (hi, lo) if (i & k) == 0 else (lo, hi)
            j //= 2
        k *= 2
    shift = SUBLANES // 2
    while shift >= 1:
        other = [pltpu.roll(v[count - 1 - i], shift, axis=0) for i in range(count)]
        v = _bitonic_merge([jnp.maximum(a, b) for a, b in zip(v, other)])
        shift //= 2
    return jnp.concatenate([a[0:1] for a in v], axis=0)


def _peer_scores_kernel(qp_ref, keys_ref, s2_ref, th_ref, e1_ref, e2_ref, th_scr, e1_scr):
    tn = qp_ref.shape[0]
    k = PEER_TOPK
    half = PEER_DKEY // 2
    rowk = lax.broadcasted_iota(jnp.int32, (k, tn), 0)
    for h in range(PEER_HEADS):
        q1 = qp_ref[:, h * PEER_DKEY:h * PEER_DKEY + half].astype(BF16)
        q2 = qp_ref[:, h * PEER_DKEY + half:(h + 1) * PEER_DKEY].astype(BF16)
        s1 = _dot_nt(keys_ref[h, 0], q1)
        s2 = _dot_nt(keys_ref[h, 1], q2)
        t1 = _top_values(s1, k)
        t2 = _top_values(s2, k)
        cands = []
        for i in range(k // 2):
            cands.append(jnp.where(rowk < k // (i + 1), t1[i:i + 1, :] + t2, NEG_INF))
        cands.append(t1[k // 2:, :] + t2[0:1, :])
        cand = jnp.concatenate(cands, axis=0)
        top = t1[0:1, :] + t2[0:1, :]
        cur = cand
        for _ in range(k - 1):
            cur = jnp.where(cur == jnp.max(cur, axis=0, keepdims=True), NEG_INF, cur)
        tau = jnp.max(cur, axis=0, keepdims=True)
        z = jnp.sum(jnp.where(cand >= tau, jnp.exp(cand - top), 0.0), axis=0, keepdims=True)
        best2 = t2[0:1, :]
        th = jnp.where(s1 + best2 >= tau, best2, jnp.inf)
        for i in range(k // 2):
            t1i = t1[i:i + 1, :]
            thr = jnp.full((1, tn), jnp.inf, F32)
            for j in range(k // (i + 1)):
                t2j = t2[j:j + 1, :]
                thr = jnp.where(t1i + t2j >= tau, t2j, thr)
            th = jnp.where(s1 == t1i, thr, th)
        s2_ref[h] = s2
        th_scr[h] = th
        e1_scr[h] = jnp.exp(s1 - t1[0:1, :])
        e2_ref[h] = jnp.exp(s2 - t2[0:1, :]) * (0.5 / z)
    th_ref[...] = jnp.transpose(th_scr[...], (1, 0, 2))
    e1_ref[...] = jnp.transpose(e1_scr[...], (1, 0, 2))


def _peer_scores(qp, keys_b, tn):
    n = qp.shape[0]
    by_head = pl.BlockSpec((PEER_HEADS, N_KEYS, tn), lambda i: (0, 0, i))
    by_key = pl.BlockSpec((N_KEYS, PEER_HEADS, tn), lambda i: (0, 0, i))
    head_shape = jax.ShapeDtypeStruct((PEER_HEADS, N_KEYS, n), F32)
    key_shape = jax.ShapeDtypeStruct((N_KEYS, PEER_HEADS, n), F32)
    return pl.pallas_call(
        _peer_scores_kernel,
        grid=(n // tn,),
        in_specs=[pl.BlockSpec((tn, PEER_HEADS * PEER_DKEY), lambda i: (i, 0)),
                  pl.BlockSpec((PEER_HEADS, 2, N_KEYS, PEER_DKEY // 2), lambda i: (0, 0, 0, 0))],
        out_specs=[by_head, by_key, by_key, by_head],
        out_shape=[head_shape, key_shape, key_shape, head_shape],
        scratch_shapes=[pltpu.VMEM((PEER_HEADS, N_KEYS, tn), F32)] * 2,
        compiler_params=_params("parallel"),
        name="peer_scores",
    )(qp, keys_b)


def _peer_kernel(x_ref, h_ref, u_ref, vt_ref, s2_ref, th_ref, e1_ref, e2_ref, nfw_ref,
                 o_ref, acc_scr, z_scr, w_scr, c0_scr, c1_scr, *, eb, tn, nblk):
    j = pl.program_id(1)
    slabs = eb // N_KEYS

    def coefficients(c_ref):
        z_scr[...] = _dot_nt(u_ref[...], x_ref[...])
        zeros = [jnp.zeros((SUBLANES, LANES), F32)] * 2
        tile = 0
        for al in range(slabs):
            a = j * slabs + al
            for tc in range(tn // LANES):
                lanes = slice(tc * LANES, (tc + 1) * LANES)
                ths = [th_ref[a, h:h + 1, lanes] for h in range(PEER_HEADS)]
                e1s = [e1_ref[a, h:h + 1, lanes] for h in range(PEER_HEADS)]
                zero = zeros[tile % 2]
                for v in range(N_KEYS // SUBLANES):
                    brow = slice(v * SUBLANES, (v + 1) * SUBLANES)
                    w = zero
                    for h in range(PEER_HEADS):
                        w = w + jnp.where(s2_ref[h, brow, lanes] >= ths[h],
                                          e2_ref[h, brow, lanes], 0.0) * e1s[h]
                    if v == 0:
                        zeros[tile % 2] = pltpu.roll(pltpu.roll(w * 0.0, 1, axis=1), 1, axis=1)
                    w_scr[al * N_KEYS + v * SUBLANES:al * N_KEYS + (v + 1) * SUBLANES, lanes] = w
                tile += 1
        rows = 2 * SUBLANES
        for r in range(eb // rows):
            erow = slice(r * rows, (r + 1) * rows)
            z = z_scr[erow, :]
            c_ref[erow, :] = (w_scr[erow, :] * (z * (1.0 + lax.erf(z * math.sqrt(0.5))))).astype(BF16)

    def accumulate(c_ref):
        acc_scr[...] += _dot(vt_ref[...], c_ref[...])

    odd = (j % 2) == 1
    inner = jnp.logical_and(j > 0, j < nblk)

    @pl.when(j == 0)
    def _():
        acc_scr[...] = jnp.zeros_like(acc_scr)
        coefficients(c0_scr)

    @pl.when(jnp.logical_and(inner, odd))
    def _():
        accumulate(c0_scr)
        coefficients(c1_scr)

    @pl.when(jnp.logical_and(inner, jnp.logical_not(odd)))
    def _():
        accumulate(c1_scr)
        coefficients(c0_scr)

    @pl.when(j == nblk)
    def _():
        accumulate(c1_scr if (nblk - 1) % 2 == 1 else c0_scr)
        hh = h_ref[...] + acc_scr[...].T
        o_ref[...] = _rms(hh, nfw_ref[...])


def _peer(xn, h, u_b, vt_b, s2, th, e1, e2, nf_w, tn, eb):
    n = xn.shape[0]
    nblk = u_b.shape[0] // eb
    tok = pl.BlockSpec((tn, D_MODEL), lambda i, j: (i, 0))
    by_head = pl.BlockSpec((PEER_HEADS, N_KEYS, tn), lambda i, j: (0, 0, i))
    by_key = pl.BlockSpec((N_KEYS, PEER_HEADS, tn), lambda i, j: (0, 0, i))
    return pl.pallas_call(
        functools.partial(_peer_kernel, eb=eb, tn=tn, nblk=nblk),
        grid=(n // tn, nblk + 1),
        in_specs=[tok, tok,
                  pl.BlockSpec((eb, D_MODEL), lambda i, j: (jnp.minimum(j, nblk - 1), 0)),
                  pl.BlockSpec((D_MODEL, eb), lambda i, j: (0, jnp.maximum(j - 1, 0))),
                  by_head, by_key, by_key, by_head,
                  pl.BlockSpec((1, D_MODEL), lambda i, j: (0, 0))],
        out_specs=tok,
        out_shape=jax.ShapeDtypeStruct((n, D_MODEL), F32),
        scratch_shapes=[pltpu.VMEM((D_MODEL, tn), F32),
                        pltpu.VMEM((eb, tn), F32),
                        pltpu.VMEM((eb, tn), F32),
                        pltpu.VMEM((eb, tn), BF16),
                        pltpu.VMEM((eb, tn), BF16)],
        compiler_params=_params("parallel", "arbitrary"),
        name="peer_experts",
    )(xn, h, u_b, vt_b, s2, th, e1, e2, nf_w)


def _tail(x, att, og, w, tm, tn, eb):
    h, xn, qp = _out_proj(x, att, og, w["w_out"], w["norm2"], w["wq"], tm)
    s2, th, e1, e2 = _peer_scores(qp, w["keys"], min(tn, 256))
    return _peer(xn, h, w["u"], w["vt"], s2, th, e1, e2, w["norm_f"], tn, eb)


def _window_view(t):
    return jnp.transpose(t, (0, 2, 3, 1))


def _window_unview(t):
    return jnp.transpose(t, (0, 3, 1, 2))


def kernel(x_prompt, x_sample, cache_k_win, cache_v_win, state_hgrn, norm1_w, w_in,
           hgrn_norm_w, hgrn_lb_logits, w_out, norm2_w, peer_wq, peer_sub_keys,
           peer_u, peer_v, norm_f_w):
    layer = 0
    batch, seq, _ = x_prompt.shape
    dec_batch, dec_seq, _ = x_sample.shape
    w_in_b = w_in[layer].astype(BF16)
    w = {
        "norm1": norm1_w[layer][None, :],
        "w_in": w_in_b,
        "w_kvt": w_in_b[:, ATT_WIDTH:3 * ATT_WIDTH].T,
        "hgrn_norm": hgrn_norm_w[layer][None, :],
        "w_out": w_out[layer].astype(BF16),
        "norm2": norm2_w[layer][None, :],
        "wq": peer_wq[layer].astype(BF16),
        "keys": peer_sub_keys[layer].astype(BF16),
        "u": peer_u[layer].astype(BF16),
        "vt": peer_v[layer].astype(BF16).T,
        "norm_f": norm_f_w[None, :],
    }
    heads = (ATT_HEADS, HEAD_DIM)

    xp = x_prompt.reshape(batch * seq, D_MODEL)
    q, k, v, g4, kt, vt = _in_proj(xp, w["norm1"], w["w_in"], w["w_kvt"], 512, seq)
    att = _attn_prompt(q, k, v, batch, seq)
    og, s_prompt = _hgrn(g4, hgrn_lb_logits, w["hgrn_norm"], None, batch, seq, seq // 2, None, layer)
    y_prompt = _tail(xp, att, og, w, 512, 512, PEER_EXPERT_BLOCK).reshape(batch, seq, D_MODEL)
    keep = min(max(wd for wd, _ in ATT_BRANCHES), seq)
    k_win_prompt = _window_unview(kt.reshape(batch, *heads, seq)[..., seq - keep:])
    v_win_prompt = _window_unview(vt.reshape(batch, *heads, seq)[..., seq - keep:])

    tms = dec_batch * dec_seq
    xs = x_sample.reshape(tms, D_MODEL)
    qs, _, _, g4s, kts, vts = _in_proj(xs, w["norm1"], w["w_in"], w["w_kvt"], tms, tms)
    rows = SUBLANES
    q_pad = jnp.pad(jnp.transpose(qs.reshape(dec_batch, dec_seq, *heads), (0, 2, 1, 3)),
                    ((0, 0), (0, 0), (0, rows - dec_seq), (0, 0)))
    new_t = lambda t: jnp.pad(
        jnp.transpose(t.reshape(*heads, dec_batch, dec_seq), (2, 0, 1, 3)),
        ((0, 0), (0, 0), (0, 0), (LANES - dec_seq, 0)))
    att_s, k_buf, v_buf = _attn_sample(q_pad, new_t(kts), new_t(vts),
                                       _window_view(cache_k_win[layer]),
                                       _window_view(cache_v_win[layer]), dec_seq)
    att_s = jnp.transpose(att_s[:, :, :dec_seq], (0, 2, 1, 3)).reshape(tms, ATT_WIDTH)
    seq_pad = HGRN_CHUNK
    g4p = jnp.pad(g4s.reshape(dec_batch, dec_seq, 4 * HGRN_WIDTH),
                  ((0, 0), (0, seq_pad - dec_seq), (0, 0))).reshape(dec_batch * seq_pad, 4 * HGRN_WIDTH)
    ogp, s_sample = _hgrn(g4p, hgrn_lb_logits, w["hgrn_norm"], state_hgrn[layer],
                          dec_batch, seq_pad, seq_pad, dec_seq, layer)
    og_s = ogp.reshape(dec_batch, seq_pad, HGRN_WIDTH)[:, :dec_seq].reshape(tms, HGRN_WIDTH)
    y_sample = _tail(xs, att_s, og_s, w, tms, tms, PEER_EXPERT_BLOCK).reshape(dec_batch, dec_seq, D_MODEL)

    return (y_prompt, y_sample,
            k_win_prompt[None], v_win_prompt[None],
            s_prompt[None],
            _window_unview(k_buf)[None], _window_unview(v_buf)[None],
            s_sample.astype(state_hgrn.dtype)[None])
```

```python
import functools
import math

import jax
import jax.numpy as jnp
from jax import lax
from jax.experimental import pallas as pl
from jax.experimental.pallas import tpu as pltpu

F32 = jnp.float32
BF16 = jnp.bfloat16

D_MODEL = 1024
HEAD_DIM = 64
ATT_WIDTH = 512
ATT_HEADS = 8
ATT_BRANCHES = ((128, 1), (512, 4), (2048, 16))
ATT_STEPS = 128
ATT_SCALE = HEAD_DIM ** -0.5
ATT_UNROLL = 8
ATT_SAMPLE_HEADS = 8
HGRN_WIDTH = 512
HGRN_DK = 128
HGRN_HEADS = 4
HGRN_CHUNK = 16
IN_COLS = 3 * ATT_WIDTH + 4 * HGRN_WIDTH
N_KEYS = 128
PEER_HEADS = 8
PEER_DKEY = 256
PEER_TOPK = 16
EPS = 1e-6
NEG_INF = float("-inf")

LANES = 128
SUBLANES = 8
PEER_EXPERT_BLOCK = SUBLANES * N_KEYS
VMEM_LIMIT = 56 * 1024 * 1024


def _dot(a, b):
    return jnp.dot(a, b, preferred_element_type=F32)


def _dot_nt(a, b):
    return lax.dot_general(a, b, (((1,), (1,)), ((), ())), preferred_element_type=F32)


def _dot_tn(a, b):
    return lax.dot_general(a, b, (((0,), (0,)), ((), ())), preferred_element_type=F32)


def _rms(x, w):
    return x * lax.rsqrt(jnp.mean(x * x, axis=-1, keepdims=True) + EPS) * w


def _params(*sem):
    return pltpu.CompilerParams(dimension_semantics=sem, vmem_limit_bytes=VMEM_LIMIT)


def _in_proj_kernel(x_ref, nw_ref, w_ref, wkv_ref, q_ref, k_ref, v_ref, g_ref, kt_ref, vt_ref):
    xn = _rms(x_ref[...], nw_ref[...]).astype(BF16)
    p = _dot(xn, w_ref[...])
    q_ref[...] = p[:, :ATT_WIDTH] * ATT_SCALE
    k_ref[...] = p[:, ATT_WIDTH:2 * ATT_WIDTH]
    v_ref[...] = p[:, 2 * ATT_WIDTH:3 * ATT_WIDTH]
    g_ref[...] = p[:, 3 * ATT_WIDTH:]
    kvt = _dot_nt(wkv_ref[...], xn)
    kt_ref[0] = kvt[:ATT_WIDTH]
    vt_ref[0] = kvt[ATT_WIDTH:]


def _in_proj(x, norm_w, w_in_b, w_kvt_b, tm, seq):
    n = x.shape[0]
    per_seq = seq // tm
    row = lambda i: (i, 0)
    fixed = lambda i: (0, 0)
    tr = lambda i: (i // per_seq, 0, i % per_seq)
    return pl.pallas_call(
        _in_proj_kernel,
        grid=(n // tm,),
        in_specs=[pl.BlockSpec((tm, D_MODEL), row),
                  pl.BlockSpec((1, D_MODEL), fixed),
                  pl.BlockSpec((D_MODEL, IN_COLS), fixed),
                  pl.BlockSpec((2 * ATT_WIDTH, D_MODEL), fixed)],
        out_specs=[pl.BlockSpec((tm, ATT_WIDTH), row),
                   pl.BlockSpec((tm, ATT_WIDTH), row),
                   pl.BlockSpec((tm, ATT_WIDTH), row),
                   pl.BlockSpec((tm, 4 * HGRN_WIDTH), row),
                   pl.BlockSpec((1, ATT_WIDTH, tm), tr),
                   pl.BlockSpec((1, ATT_WIDTH, tm), tr)],
        out_shape=[jax.ShapeDtypeStruct((n, ATT_WIDTH), F32),
                   jax.ShapeDtypeStruct((n, ATT_WIDTH), F32),
                   jax.ShapeDtypeStruct((n, ATT_WIDTH), F32),
                   jax.ShapeDtypeStruct((n, 4 * HGRN_WIDTH), F32),
                   jax.ShapeDtypeStruct((n // seq, ATT_WIDTH, seq), F32),
                   jax.ShapeDtypeStruct((n // seq, ATT_WIDTH, seq), F32)],
        compiler_params=_params("parallel"),
        name="in_proj",
    )(x, norm_w, w_in_b, w_kvt_b)


def _attn_prompt_kernel(q_ref, k_ref, v_ref, o_ref, qc_scr, kc_scr, vc_scr, m_scr, l_scr, a_scr,
                        *, seq):
    blk = ATT_STEPS
    (_, d0), (_, d1), (_, d2) = ATT_BRANCHES
    assert d0 == 1 and d2 % d1 == 0
    cls = seq // d1
    sub = d2 // d1
    lo = lax.broadcasted_iota(jnp.int32, (blk, LANES), 1) < HEAD_DIM

    for r in range(d1):
        src = pl.ds(r, cls, stride=d1)
        dst = pl.ds(r * cls, cls)
        qc_scr[dst, :] = q_ref[src, :]
        kc_scr[dst, :] = k_ref[src, :]
        vc_scr[dst, :] = v_ref[src, :]

    def attend(srcs, br, stride, q_start, k_start, n_keys, off):
        q_src, k_src, v_src = srcs
        step = {} if stride == 1 else {"stride": stride}
        qrows = pl.ds(q_start, blk, **step)
        krows = pl.ds(k_start, n_keys, **step)
        qb = q_src[qrows, :]
        kb = k_src[krows, :].astype(BF16)
        vb = v_src[krows, :]
        dist = (off + lax.broadcasted_iota(jnp.int32, (blk, n_keys), 0)
                - lax.broadcasted_iota(jnp.int32, (blk, n_keys), 1))
        ok = lax.bitcast_convert_type(dist, jnp.uint32) <= blk
        lo_k = lax.broadcasted_iota(jnp.int32, (n_keys, LANES), 1) < HEAD_DIM
        ms, ps, vhs = [], [], []
        for first_head in (True, False):
            qh = (jnp.where(lo, qb, 0.0) if first_head else jnp.where(lo, 0.0, qb)).astype(BF16)
            s = jnp.where(ok, _dot_nt(qh, kb), NEG_INF)
            m = jnp.max(s, axis=-1, keepdims=True)
            ms.append(m)
            ps.append(jnp.exp(s - m).astype(BF16))
            vhs.append((jnp.where(lo_k, vb, 1.0) if first_head else jnp.where(lo_k, 1.0, vb)).astype(BF16))
        return qrows, ms, ps, vhs

    def finish(br, qrows, ms, ps, vhs):
        rs = [_dot(p, vh) for p, vh in zip(ps, vhs)]
        m_scr[br, qrows, :] = jnp.where(lo, ms[0], ms[1])
        a_scr[br, qrows, :] = jnp.where(lo, rs[0], rs[1])
        l_scr[br, qrows, :] = jnp.where(lo, rs[1], rs[0])

    def run_branch(srcs, br, stride, n_classes, class_base, class_len):
        nb = class_len // blk

        def one(idx):
            n = idx // n_classes
            c = idx - n * n_classes
            base = class_base(c)
            if nb == 1:
                return attend(srcs, br, stride, base, base, blk, 0)
            first = n == 0
            q_start = base + n * (blk * stride)
            k_start = base + jnp.where(first, 0, n - 1) * (blk * stride)
            return attend(srcs, br, stride, q_start, k_start, 2 * blk, jnp.where(first, 0, blk))

        def body(trip, carry):
            held = [one(trip * ATT_UNROLL + g) for g in range(ATT_UNROLL)]
            for parts in held:
                finish(br, *parts)
            return carry

        lax.fori_loop(0, nb * n_classes // ATT_UNROLL, body, 0)

    natural = (q_ref, k_ref, v_ref)
    by_class = (qc_scr, kc_scr, vc_scr)
    run_branch(natural, 0, 1, 1, lambda c: 0, seq)
    run_branch(by_class, 1, 1, d1, lambda c: c * cls, cls)
    run_branch(by_class, 2, sub, d2, lambda c: (c // sub) * cls + c % sub, cls // sub)

    chunk = 256
    per_class = cls // chunk

    def combine(idx, carry):
        r = idx // per_class
        c = idx - r * per_class
        nat = pl.ds(r + c * (chunk * d1), chunk, stride=d1)
        byc = pl.ds(pl.multiple_of(r * cls + c * chunk, chunk), chunk)
        rows = (nat, byc, byc)
        m = [m_scr[b, rows[b], :] for b in range(3)]
        mx = jnp.maximum(jnp.maximum(m[0], m[1]), m[2])
        den = jnp.zeros((chunk, LANES), F32)
        num = jnp.zeros((chunk, LANES), F32)
        for b in range(3):
            w = jnp.exp(m[b] - mx)
            den = den + w * pltpu.roll(l_scr[b, rows[b], :], HEAD_DIM, axis=1)
            num = num + w * a_scr[b, rows[b], :]
        o_ref[nat, :] = num / den
        return carry

    lax.fori_loop(0, d1 * per_class, combine, 0)


def _attn_prompt(q, k, v, batch, seq):
    spec = pl.BlockSpec((seq, LANES), lambda b, hp: (b, hp))
    return pl.pallas_call(
        functools.partial(_attn_prompt_kernel, seq=seq),
        grid=(batch, ATT_WIDTH // LANES),
        in_specs=[spec, spec, spec],
        out_specs=spec,
        out_shape=jax.ShapeDtypeStruct((batch * seq, ATT_WIDTH), F32),
        scratch_shapes=[pltpu.VMEM((seq, LANES), F32)] * 3 + [pltpu.VMEM((3, seq, LANES), F32)] * 3,
        compiler_params=_params("parallel", "parallel"),
        name="attn_prompt",
    )(q, k, v)


def _attn_sample_kernel(q_ref, kn_ref, vn_ref, kc_ref, vc_ref, o_ref, ko_ref, vo_ref,
                        *, n_new, n_buf):
    for g in range(q_ref.shape[1]):
        _attn_sample_head(q_ref.at[0, g], kn_ref.at[0, g], vn_ref.at[0, g], kc_ref.at[0, g],
                          vc_ref.at[0, g], o_ref.at[0, g], ko_ref.at[0, g], vo_ref.at[0, g],
                          n_new=n_new, n_buf=n_buf)


def _attn_sample_head(q_ref, kn_ref, vn_ref, kc_ref, vc_ref, o_ref, ko_ref, vo_ref, *, n_new, n_buf):
    rows = q_ref.shape[0]
    q = q_ref[...].astype(BF16)
    kc = kc_ref[...]
    vc = vc_ref[...]
    kn = kn_ref[...]
    vn = vn_ref[...]
    s_c = _dot(q, kc.astype(BF16))
    s_n = _dot(q, kn.astype(BF16))
    tok_c = lax.broadcasted_iota(jnp.int32, (rows, n_buf), 0)
    dist_c = n_buf + tok_c - lax.broadcasted_iota(jnp.int32, (rows, n_buf), 1)
    tok_n = lax.broadcasted_iota(jnp.int32, (rows, LANES), 0)
    key_n = lax.broadcasted_iota(jnp.int32, (rows, LANES), 1) - (LANES - n_new)
    dist_n = tok_n - key_n
    parts = []
    for window, dil in ATT_BRANCHES:
        ok_c = jnp.where((dist_c & (dil - 1)) == 0, dist_c, window + 1) <= window
        ok_n = lax.bitcast_convert_type(
            jnp.where((dist_n & (dil - 1)) == 0, jnp.where(key_n >= 0, dist_n, -1), -1),
            jnp.uint32) <= window
        sc = jnp.where(ok_c, s_c, NEG_INF)
        sn = jnp.where(ok_n, s_n, NEG_INF)
        m = jnp.maximum(jnp.max(sc, axis=-1, keepdims=True), jnp.max(sn, axis=-1, keepdims=True))
        pc = jnp.exp(sc - m)
        pn = jnp.exp(sn - m)
        l = jnp.sum(pc, axis=-1, keepdims=True) + jnp.sum(pn, axis=-1, keepdims=True)
        acc = (_dot_nt(pc.astype(BF16), vc.astype(BF16))
               + _dot_nt(pn.astype(BF16), vn.astype(BF16)))
        parts.append((m, l, acc))
    mx = jnp.maximum(jnp.maximum(parts[0][0], parts[1][0]), parts[2][0])
    den = jnp.zeros((rows, 1), F32)
    num = jnp.zeros((rows, HEAD_DIM), F32)
    for m, l, acc in parts:
        w = jnp.exp(m - mx)
        den = den + w * l
        num = num + w * acc
    o_ref[...] = num / den
    fresh = lax.broadcasted_iota(jnp.int32, (HEAD_DIM, LANES), 1) >= LANES - n_new
    for c_val, n_val, out_ref in ((kc, kn, ko_ref), (vc, vn, vo_ref)):
        shifted = pltpu.roll(c_val, n_buf - n_new, axis=1)
        out_ref[:, :n_buf - LANES] = shifted[:, :n_buf - LANES]
        out_ref[:, n_buf - LANES:] = jnp.where(fresh, n_val, shifted[:, n_buf - LANES:])


def _attn_sample(q_pad, kn_t, vn_t, cache_kt, cache_vt, n_new):
    batch, heads, rows, _ = q_pad.shape
    n_buf = cache_kt.shape[3]
    idx = lambda b, h: (b, h, 0, 0)
    group = ATT_SAMPLE_HEADS
    buf_spec = pl.BlockSpec((1, group, HEAD_DIM, n_buf), idx)
    new_spec = pl.BlockSpec((1, group, HEAD_DIM, LANES), idx)
    q_spec = pl.BlockSpec((1, group, rows, HEAD_DIM), idx)
    buf_shape = jax.ShapeDtypeStruct((batch, heads, HEAD_DIM, n_buf), F32)
    return pl.pallas_call(
        functools.partial(_attn_sample_kernel, n_new=n_new, n_buf=n_buf),
        grid=(batch, heads // group),
        in_specs=[q_spec, new_spec, new_spec, buf_spec, buf_spec],
        out_specs=[q_spec, buf_spec, buf_spec],
        out_shape=[jax.ShapeDtypeStruct((batch, heads, rows, HEAD_DIM), F32), buf_shape, buf_shape],
        compiler_params=_params("parallel", "parallel"),
        name="attn_sample",
    )(q_pad, kn_t, vn_t, cache_kt, cache_vt)


def _hgrn_kernel(*refs, ts, n_valid, has_state, layer):
    if has_state:
        hq_ref, hf_ref, hi_ref, hg_ref, lbl_ref, nw_ref, s0_ref, o_ref, sfin_ref, st_scr = refs
    else:
        hq_ref, hf_ref, hi_ref, hg_ref, lbl_ref, nw_ref, o_ref, sfin_ref, st_scr = refs
    C = HGRN_CHUNK
    stretch = pl.program_id(1)

    @pl.when(stretch == 0)
    def _():
        for h in range(HGRN_HEADS):
            if has_state:
                st_scr[h] = s0_ref[0, h].T
            else:
                st_scr[h] = jnp.zeros((HGRN_DK, HGRN_DK), F32)

    lg = lbl_ref[...]
    ex = jnp.exp(lg - jnp.max(lg, axis=0, keepdims=True))
    lb_all = jnp.sum(ex[:layer + 1], axis=0, keepdims=True) / jnp.sum(ex, axis=0, keepdims=True)
    nw = nw_ref[...]
    rowi = lax.broadcasted_iota(jnp.int32, (C, HGRN_DK), 0)
    ones = jnp.ones((HGRN_DK, LANES), BF16)
    base = stretch * ts

    def one_head(h, r0, rows):
        cols = slice(h * HGRN_DK, (h + 1) * HGRN_DK)
        lb = lb_all[:, cols]
        f = lb + (1.0 - lb) * jax.nn.sigmoid(hf_ref[rows, cols])
        logf = jnp.log(f)
        kk = 1.0 - f
        if n_valid is not None:
            live = (rowi + (base + r0)) < n_valid
            logf = jnp.where(live, logf, 0.0)
            kk = jnp.where(live, kk, 0.0)
        q = jax.nn.silu(hq_ref[rows, cols])
        ii = hi_ref[rows, cols]
        b = logf
        sh = 1
        while sh < C:
            b = b + jnp.where(rowi >= sh, pltpu.roll(b, sh, axis=0), 0.0)
            sh *= 2
        st = st_scr[h]
        inter = _dot_nt((q * jnp.exp(b)).astype(BF16), st.astype(BF16))
        live_rows = [min(C, SUBLANES * (t // SUBLANES + 1)) for t in range(C)]
        prods = []
        for t, nr in enumerate(live_rows):
            e = jnp.exp(jnp.where(rowi[:nr] <= t, b[t:t + 1, :] - b[:nr], NEG_INF))
            prods.append(q[t:t + 1, :] * e * kk[:nr])
        a_rep = _dot(jnp.concatenate(prods, axis=0).astype(BF16), ones)
        intra, off = [], 0
        for nr in live_rows:
            intra.append(jnp.sum(a_rep[off:off + nr, :] * ii[:nr], axis=0, keepdims=True))
            off += nr
        o = inter + jnp.concatenate(intra, axis=0)
        b_last = b[C - 1:C, :]
        kdec = kk * jnp.exp(b_last - b)
        st_scr[h] = st * jnp.exp(b_last) + _dot_tn(ii.astype(BF16), kdec.astype(BF16))
        o_ref[rows, cols] = _rms(o, nw) * jax.nn.silu(hg_ref[rows, cols])

    def body(c, carry):
        r0 = pl.multiple_of(c * C, C)
        rows = pl.ds(r0, C)
        for h in range(HGRN_HEADS):
            one_head(h, r0, rows)
        return carry

    lax.fori_loop(0, ts // C, body, 0, unroll=min(4, ts // C))

    @pl.when(stretch == pl.num_programs(1) - 1)
    def _():
        for h in range(HGRN_HEADS):
            sfin_ref[0, h] = st_scr[h].T


def _hgrn(g4, lb_logits, norm_w, state, batch, seq, ts, n_valid, layer):
    has_state = state is not None
    nh = HGRN_HEADS
    per_seq = seq // ts
    col = lambda part: pl.BlockSpec((ts, HGRN_WIDTH), lambda b, s, part=part: (b * per_seq + s, part))
    st_spec = pl.BlockSpec((1, nh, HGRN_DK, HGRN_DK), lambda b, s: (b, 0, 0, 0))
    in_specs = [col(0), col(1), col(2), col(3),
                pl.BlockSpec(lb_logits.shape, lambda b, s: (0, 0)),
                pl.BlockSpec((1, HGRN_DK), lambda b, s: (0, 0))]
    args = [g4, g4, g4, g4, lb_logits, norm_w]
    if has_state:
        in_specs.append(st_spec)
        args.append(state)
    return pl.pallas_call(
        functools.partial(_hgrn_kernel, ts=ts, n_valid=n_valid, has_state=has_state, layer=layer),
        grid=(batch, per_seq),
        in_specs=in_specs,
        out_specs=[pl.BlockSpec((ts, HGRN_WIDTH), lambda b, s: (b * per_seq + s, 0)), st_spec],
        out_shape=[jax.ShapeDtypeStruct((batch * seq, HGRN_WIDTH), F32),
                   jax.ShapeDtypeStruct((batch, nh, HGRN_DK, HGRN_DK), F32)],
        scratch_shapes=[pltpu.VMEM((nh, HGRN_DK, HGRN_DK), F32)],
        compiler_params=_params("parallel", "arbitrary"),
        name="hgrn",
    )(*args)


def _out_proj_kernel(x_ref, att_ref, og_ref, wo_ref, nw_ref, wq_ref, h_ref, xn_ref, qp_ref):
    h = (x_ref[...]
         + _dot(att_ref[...].astype(BF16), wo_ref[:ATT_WIDTH, :])
         + _dot(og_ref[...].astype(BF16), wo_ref[ATT_WIDTH:, :]))
    h_ref[...] = h
    xn = _rms(h, nw_ref[...]).astype(BF16)
    xn_ref[...] = xn
    qp_ref[...] = _dot(xn, wq_ref[...])


def _out_proj(x, att, og, w_out_b, norm_w, wq_b, tm):
    n = x.shape[0]
    qcols = wq_b.shape[1]
    row = lambda i: (i, 0)
    fixed = lambda i: (0, 0)
    return pl.pallas_call(
        _out_proj_kernel,
        grid=(n // tm,),
        in_specs=[pl.BlockSpec((tm, D_MODEL), row),
                  pl.BlockSpec((tm, ATT_WIDTH), row),
                  pl.BlockSpec((tm, HGRN_WIDTH), row),
                  pl.BlockSpec((ATT_WIDTH + HGRN_WIDTH, D_MODEL), fixed),
                  pl.BlockSpec((1, D_MODEL), fixed),
                  pl.BlockSpec((D_MODEL, qcols), fixed)],
        out_specs=[pl.BlockSpec((tm, D_MODEL), row),
                   pl.BlockSpec((tm, D_MODEL), row),
                   pl.BlockSpec((tm, qcols), row)],
        out_shape=[jax.ShapeDtypeStruct((n, D_MODEL), F32),
                   jax.ShapeDtypeStruct((n, D_MODEL), BF16),
                   jax.ShapeDtypeStruct((n, qcols), F32)],
        compiler_params=_params("parallel"),
        name="out_proj",
    )(x, att, og, w_out_b, norm_w, wq_b)


def _bitonic_merge(v):
    n = len(v)
    j = n // 2
    while j >= 1:
        for i in range(n):
            l = i ^ j
            if l > i:
                v[i], v[l] = jnp.maximum(v[i], v[l]), jnp.minimum(v[i], v[l])
        j //= 2
    return v


def _top_values(s, count):
    assert s.shape[0] == count * SUBLANES and count & (count - 1) == 0
    v = [s[i * SUBLANES:(i + 1) * SUBLANES] for i in range(count)]
    k = 2
    while k <= count:
        j = k // 2
        while j >= 1:
            for i in range(count):
                l = i ^ j
                if l > i:
                    hi, lo = jnp.maximum(v[i], v[l]), jnp.minimum(v[i], v[l])
                    v[i], v[l] = (hi, lo) if (i & k) == 0 else (lo, hi)
            j //= 2
        k *= 2
    shift = SUBLANES // 2
    while shift >= 1:
        other = [pltpu.roll(v[count - 1 - i], shift, axis=0) for i in range(count)]
        v = _bitonic_merge([jnp.maximum(a, b) for a, b in zip(v, other)])
        shift //= 2
    return jnp.concatenate([a[0:1] for a in v], axis=0)


def _peer_scores_kernel(qp_ref, keys_ref, s2_ref, th_ref, e1_ref, e2_ref, th_scr, e1_scr):
    tn = qp_ref.shape[0]
    k = PEER_TOPK
    half = PEER_DKEY // 2
    for h in range(PEER_HEADS):
        q1 = qp_ref[:, h * PEER_DKEY:h * PEER_DKEY + half].astype(BF16)
        q2 = qp_ref[:, h * PEER_DKEY + half:(h + 1) * PEER_DKEY].astype(BF16)
        s1 = _dot_nt(keys_ref[h, 0], q1)
        s2 = _dot_nt(keys_ref[h, 1], q2)
        t1 = _top_values(s1, k)
        t2 = _top_values(s2, k)
        half_k = k // 2
        cands = []
        for i in range(half_k - 1):
            cands += [t1[i:i + 1, :] + t2[:half_k, :], t1[i:i + 1, :] + t2[half_k:, :]]
        cands += [t1[half_k - 1:half_k, :] + t2[:half_k, :], t1[half_k:, :] + t2[0:1, :]]
        chosen = _top_values(jnp.concatenate(cands, axis=0), k)
        tau = chosen[k - 1:k, :]
        z = jnp.sum(jnp.exp(chosen - chosen[0:1, :]), axis=0, keepdims=True)
        best2 = t2[0:1, :]
        th = jnp.where(s1 + best2 >= tau, best2, jnp.inf)
        for i in range(k // 2):
            t1i = t1[i:i + 1, :]
            thr = jnp.full((1, tn), jnp.inf, F32)
            for j in range(k // (i + 1)):
                t2j = t2[j:j + 1, :]
                thr = jnp.where(t1i + t2j >= tau, t2j, thr)
            th = jnp.where(s1 == t1i, thr, th)
        s2_ref[h] = s2
        th_scr[h] = th
        e1_scr[h] = jnp.exp(s1 - t1[0:1, :])
        e2_ref[h] = jnp.exp(s2 - t2[0:1, :]) * (0.5 / z)
    th_ref[...] = jnp.transpose(th_scr[...], (1, 0, 2))
    e1_ref[...] = jnp.transpose(e1_scr[...], (1, 0, 2))


def _peer_scores(qp, keys_b, tn):
    n = qp.shape[0]
    by_head = pl.BlockSpec((PEER_HEADS, N_KEYS, tn), lambda i: (0, 0, i))
    by_key = pl.BlockSpec((N_KEYS, PEER_HEADS, tn), lambda i: (0, 0, i))
    head_shape = jax.ShapeDtypeStruct((PEER_HEADS, N_KEYS, n), F32)
    key_shape = jax.ShapeDtypeStruct((N_KEYS, PEER_HEADS, n), F32)
    return pl.pallas_call(
        _peer_scores_kernel,
        grid=(n // tn,),
        in_specs=[pl.BlockSpec((tn, PEER_HEADS * PEER_DKEY), lambda i: (i, 0)),
                  pl.BlockSpec((PEER_HEADS, 2, N_KEYS, PEER_DKEY // 2), lambda i: (0, 0, 0, 0))],
        out_specs=[by_head, by_key, by_key, by_head],
        out_shape=[head_shape, key_shape, key_shape, head_shape],
        scratch_shapes=[pltpu.VMEM((PEER_HEADS, N_KEYS, tn), F32)] * 2,
        compiler_params=_params("parallel"),
        name="peer_scores",
    )(qp, keys_b)


def _peer_kernel(x_ref, h_ref, u_ref, vt_ref, s2_in, th_ref, e1_ref, e2_in, nfw_ref,
                 o_ref, acc_scr, z_scr, w_scr, c0_scr, c1_scr, s2_ref, e2_ref, *, eb, tn, nblk):
    j = pl.program_id(1)
    slabs = eb // N_KEYS

    def coefficients(c_ref):
        z_scr[...] = _dot_nt(u_ref[...], x_ref[...])
        zeros = [jnp.zeros((SUBLANES, LANES), F32)] * 2
        tile = 0
        for al in range(slabs):
            a = j * slabs + al
            for tc in range(tn // LANES):
                lanes = slice(tc * LANES, (tc + 1) * LANES)
                ths = [th_ref[a, h:h + 1, lanes] for h in range(PEER_HEADS)]
                e1s = [e1_ref[a, h:h + 1, lanes] for h in range(PEER_HEADS)]
                zero = zeros[tile % 2]
                for v in range(N_KEYS // SUBLANES):
                    brow = slice(v * SUBLANES, (v + 1) * SUBLANES)
                    w = zero
                    for h in range(PEER_HEADS):
                        w = w + jnp.where(s2_ref[tc, h, brow, :] >= ths[h],
                                          e2_ref[tc, h, brow, :], 0.0) * e1s[h]
                    if v == 0:
                        zeros[tile % 2] = pltpu.roll(pltpu.roll(w * 0.0, 1, axis=1), 1, axis=1)
                    w_scr[tc, al * N_KEYS + v * SUBLANES:al * N_KEYS + (v + 1) * SUBLANES, :] = w
                tile += 1
        rows = 2 * SUBLANES
        for r in range(eb // rows):
            erow = slice(r * rows, (r + 1) * rows)
            z = z_scr[erow, :]
            w = jnp.concatenate([w_scr[tc, erow, :] for tc in range(tn // LANES)], axis=1)
            c_ref[erow, :] = (w * (z * (1.0 + lax.erf(z * math.sqrt(0.5))))).astype(BF16)

    def accumulate(c_ref):
        acc_scr[...] += _dot(vt_ref[...], c_ref[...])

    odd = (j % 2) == 1
    inner = jnp.logical_and(j > 0, j < nblk)

    @pl.when(j == 0)
    def _():
        for tc in range(tn // LANES):
            s2_ref[tc] = s2_in[:, :, tc * LANES:(tc + 1) * LANES]
            e2_ref[tc] = e2_in[:, :, tc * LANES:(tc + 1) * LANES]
        acc_scr[...] = jnp.zeros_like(acc_scr)
        coefficients(c0_scr)

    @pl.when(jnp.logical_and(inner, odd))
    def _():
        accumulate(c0_scr)
        coefficients(c1_scr)

    @pl.when(jnp.logical_and(inner, jnp.logical_not(odd)))
    def _():
        accumulate(c1_scr)
        coefficients(c0_scr)

    @pl.when(j == nblk)
    def _():
        accumulate(c1_scr if (nblk - 1) % 2 == 1 else c0_scr)
        hh = h_ref[...] + acc_scr[...].T
        o_ref[...] = _rms(hh, nfw_ref[...])


def _peer(xn, h, u_b, vt_b, s2, th, e1, e2, nf_w, tn, eb):
    n = xn.shape[0]
    nblk = u_b.shape[0] // eb
    tok = pl.BlockSpec((tn, D_MODEL), lambda i, j: (i, 0))
    by_head = pl.BlockSpec((PEER_HEADS, N_KEYS, tn), lambda i, j: (0, 0, i))
    by_key = pl.BlockSpec((N_KEYS, PEER_HEADS, tn), lambda i, j: (0, 0, i))
    return pl.pallas_call(
        functools.partial(_peer_kernel, eb=eb, tn=tn, nblk=nblk),
        grid=(n // tn, nblk + 1),
        in_specs=[tok, tok,
                  pl.BlockSpec((eb, D_MODEL), lambda i, j: (jnp.minimum(j, nblk - 1), 0)),
                  pl.BlockSpec((D_MODEL, eb), lambda i, j: (0, jnp.maximum(j - 1, 0))),
                  by_head, by_key, by_key, by_head,
                  pl.BlockSpec((1, D_MODEL), lambda i, j: (0, 0))],
        out_specs=tok,
        out_shape=jax.ShapeDtypeStruct((n, D_MODEL), F32),
        scratch_shapes=[pltpu.VMEM((D_MODEL, tn), F32),
                        pltpu.VMEM((eb, tn), F32),
                        pltpu.VMEM((tn // LANES, eb, LANES), F32),
                        pltpu.VMEM((eb, tn), BF16),
                        pltpu.VMEM((eb, tn), BF16),
                        pltpu.VMEM((tn // LANES, PEER_HEADS, N_KEYS, LANES), F32),
                        pltpu.VMEM((tn // LANES, PEER_HEADS, N_KEYS, LANES), F32)],
        compiler_params=_params("parallel", "arbitrary"),
        name="peer_experts",
    )(xn, h, u_b, vt_b, s2, th, e1, e2, nf_w)


def _tail(x, att, og, w, tm, tn, eb):
    h, xn, qp = _out_proj(x, att, og, w["w_out"], w["norm2"], w["wq"], tm)
    s2, th, e1, e2 = _peer_scores(qp, w["keys"], min(tn, 256))
    return _peer(xn, h, w["u"], w["vt"], s2, th, e1, e2, w["norm_f"], tn, eb)


def _window_view(t):
    return jnp.transpose(t, (0, 2, 3, 1))


def _window_unview(t):
    return jnp.transpose(t, (0, 3, 1, 2))


def kernel(x_prompt, x_sample, cache_k_win, cache_v_win, state_hgrn, norm1_w, w_in,
           hgrn_norm_w, hgrn_lb_logits, w_out, norm2_w, peer_wq, peer_sub_keys,
           peer_u, peer_v, norm_f_w):
    layer = 0
    batch, seq, _ = x_prompt.shape
    dec_batch, dec_seq, _ = x_sample.shape
    w_in_b = w_in[layer].astype(BF16)
    w = {
        "norm1": norm1_w[layer][None, :],
        "w_in": w_in_b,
        "w_kvt": w_in_b[:, ATT_WIDTH:3 * ATT_WIDTH].T,
        "hgrn_norm": hgrn_norm_w[layer][None, :],
        "w_out": w_out[layer].astype(BF16),
        "norm2": norm2_w[layer][None, :],
        "wq": peer_wq[layer].astype(BF16),
        "keys": peer_sub_keys[layer].astype(BF16),
        "u": peer_u[layer].astype(BF16),
        "vt": peer_v[layer].astype(BF16).T,
        "norm_f": norm_f_w[None, :],
    }
    heads = (ATT_HEADS, HEAD_DIM)

    xp = x_prompt.reshape(batch * seq, D_MODEL)
    q, k, v, g4, kt, vt = _in_proj(xp, w["norm1"], w["w_in"], w["w_kvt"], 512, seq)
    att = _attn_prompt(q, k, v, batch, seq)
    og, s_prompt = _hgrn(g4, hgrn_lb_logits, w["hgrn_norm"], None, batch, seq, seq // 2, None, layer)
    y_prompt = _tail(xp, att, og, w, 512, 512, PEER_EXPERT_BLOCK).reshape(batch, seq, D_MODEL)
    keep = min(max(wd for wd, _ in ATT_BRANCHES), seq)
    k_win_prompt = _window_unview(kt.reshape(batch, *heads, seq)[..., seq - keep:])
    v_win_prompt = _window_unview(vt.reshape(batch, *heads, seq)[..., seq - keep:])

    tms = dec_batch * dec_seq
    xs = x_sample.reshape(tms, D_MODEL)
    qs, _, _, g4s, kts, vts = _in_proj(xs, w["norm1"], w["w_in"], w["w_kvt"], tms, tms)
    rows = SUBLANES
    q_pad = jnp.pad(jnp.transpose(qs.reshape(dec_batch, dec_seq, *heads), (0, 2, 1, 3)),
                    ((0, 0), (0, 0), (0, rows - dec_seq), (0, 0)))
    new_t = lambda t: jnp.pad(
        jnp.transpose(t.reshape(*heads, dec_batch, dec_seq), (2, 0, 1, 3)),
        ((0, 0), (0, 0), (0, 0), (LANES - dec_seq, 0)))
    att_s, k_buf, v_buf = _attn_sample(q_pad, new_t(kts), new_t(vts),
                                       _window_view(cache_k_win[layer]),
                                       _window_view(cache_v_win[layer]), dec_seq)
    att_s = jnp.transpose(att_s[:, :, :dec_seq], (0, 2, 1, 3)).reshape(tms, ATT_WIDTH)
    seq_pad = HGRN_CHUNK
    g4p = jnp.pad(g4s.reshape(dec_batch, dec_seq, 4 * HGRN_WIDTH),
                  ((0, 0), (0, seq_pad - dec_seq), (0, 0))).reshape(dec_batch * seq_pad, 4 * HGRN_WIDTH)
    ogp, s_sample = _hgrn(g4p, hgrn_lb_logits, w["hgrn_norm"], state_hgrn[layer],
                          dec_batch, seq_pad, seq_pad, dec_seq, layer)
    og_s = ogp.reshape(dec_batch, seq_pad, HGRN_WIDTH)[:, :dec_seq].reshape(tms, HGRN_WIDTH)
    y_sample = _tail(xs, att_s, og_s, w, tms, tms, PEER_EXPERT_BLOCK).reshape(dec_batch, dec_seq, D_MODEL)

    return (y_prompt, y_sample,
            k_win_prompt[None], v_win_prompt[None],
            s_prompt[None],
            _window_unview(k_buf)[None], _window_unview(v_buf)[None],
            s_sample.astype(state_hgrn.dtype)[None])
```

```python
import functools
import math

import jax
import jax.numpy as jnp
from jax import lax
from jax.experimental import pallas as pl
from jax.experimental.pallas import tpu as pltpu

F32 = jnp.float32
BF16 = jnp.bfloat16

D_MODEL = 1024
HEAD_DIM = 64
ATT_WIDTH = 512
ATT_HEADS = 8
ATT_BRANCHES = ((128, 1), (512, 4), (2048, 16))
ATT_STEPS = 128
ATT_SCALE = HEAD_DIM ** -0.5
ATT_UNROLL = 8
ATT_SAMPLE_HEADS = 8
HGRN_WIDTH = 512
HGRN_DK = 128
HGRN_HEADS = 4
HGRN_CHUNK = 16
IN_COLS = 3 * ATT_WIDTH + 4 * HGRN_WIDTH
N_KEYS = 128
PEER_HEADS = 8
PEER_DKEY = 256
PEER_TOPK = 16
EPS = 1e-6
NEG_INF = float("-inf")

LANES = 128
SUBLANES = 8
PEER_EXPERT_BLOCK = SUBLANES * N_KEYS
PEER_SLAB_PAD = SUBLANES
VMEM_LIMIT = 56 * 1024 * 1024


def _dot(a, b):
    return jnp.dot(a, b, preferred_element_type=F32)


def _dot_nt(a, b):
    return lax.dot_general(a, b, (((1,), (1,)), ((), ())), preferred_element_type=F32)


def _dot_tn(a, b):
    return lax.dot_general(a, b, (((0,), (0,)), ((), ())), preferred_element_type=F32)


def _rms(x, w):
    return x * lax.rsqrt(jnp.mean(x * x, axis=-1, keepdims=True) + EPS) * w


def _params(*sem):
    return pltpu.CompilerParams(dimension_semantics=sem, vmem_limit_bytes=VMEM_LIMIT)


def _in_proj_kernel(x_ref, nw_ref, w_ref, wkv_ref, q_ref, k_ref, v_ref, g_ref, kt_ref, vt_ref):
    xn = _rms(x_ref[...], nw_ref[...]).astype(BF16)
    p = _dot(xn, w_ref[...])
    q_ref[...] = p[:, :ATT_WIDTH] * ATT_SCALE
    k_ref[...] = p[:, ATT_WIDTH:2 * ATT_WIDTH]
    v_ref[...] = p[:, 2 * ATT_WIDTH:3 * ATT_WIDTH]
    g_ref[...] = p[:, 3 * ATT_WIDTH:]
    kvt = _dot_nt(wkv_ref[...], xn)
    kt_ref[0] = kvt[:ATT_WIDTH]
    vt_ref[0] = kvt[ATT_WIDTH:]


def _in_proj(x, norm_w, w_in_b, w_kvt_b, tm, seq):
    n = x.shape[0]
    per_seq = seq // tm
    row = lambda i: (i, 0)
    fixed = lambda i: (0, 0)
    tr = lambda i: (i // per_seq, 0, i % per_seq)
    return pl.pallas_call(
        _in_proj_kernel,
        grid=(n // tm,),
        in_specs=[pl.BlockSpec((tm, D_MODEL), row),
                  pl.BlockSpec((1, D_MODEL), fixed),
                  pl.BlockSpec((D_MODEL, IN_COLS), fixed),
                  pl.BlockSpec((2 * ATT_WIDTH, D_MODEL), fixed)],
        out_specs=[pl.BlockSpec((tm, ATT_WIDTH), row),
                   pl.BlockSpec((tm, ATT_WIDTH), row),
                   pl.BlockSpec((tm, ATT_WIDTH), row),
                   pl.BlockSpec((tm, 4 * HGRN_WIDTH), row),
                   pl.BlockSpec((1, ATT_WIDTH, tm), tr),
                   pl.BlockSpec((1, ATT_WIDTH, tm), tr)],
        out_shape=[jax.ShapeDtypeStruct((n, ATT_WIDTH), F32),
                   jax.ShapeDtypeStruct((n, ATT_WIDTH), F32),
                   jax.ShapeDtypeStruct((n, ATT_WIDTH), F32),
                   jax.ShapeDtypeStruct((n, 4 * HGRN_WIDTH), F32),
                   jax.ShapeDtypeStruct((n // seq, ATT_WIDTH, seq), F32),
                   jax.ShapeDtypeStruct((n // seq, ATT_WIDTH, seq), F32)],
        compiler_params=_params("parallel"),
        name="in_proj",
    )(x, norm_w, w_in_b, w_kvt_b)


def _attn_prompt_kernel(q_ref, k_ref, v_ref, o_ref, qc_scr, kc_scr, vc_scr, m_scr, l_scr, a_scr,
                        *, seq):
    blk = ATT_STEPS
    (_, d0), (_, d1), (_, d2) = ATT_BRANCHES
    assert d0 == 1 and d2 % d1 == 0
    cls = seq // d1
    sub = d2 // d1
    lo = lax.broadcasted_iota(jnp.int32, (blk, LANES), 1) < HEAD_DIM

    for r in range(d1):
        src = pl.ds(r, cls, stride=d1)
        dst = pl.ds(r * cls, cls)
        qc_scr[dst, :] = q_ref[src, :]
        kc_scr[dst, :] = k_ref[src, :]
        vc_scr[dst, :] = v_ref[src, :]

    def attend(srcs, br, stride, q_start, k_start, n_keys, off):
        q_src, k_src, v_src = srcs
        step = {} if stride == 1 else {"stride": stride}
        qrows = pl.ds(q_start, blk, **step)
        krows = pl.ds(k_start, n_keys, **step)
        qb = q_src[qrows, :]
        kb = k_src[krows, :].astype(BF16)
        vb = v_src[krows, :]
        dist = (off + lax.broadcasted_iota(jnp.int32, (blk, n_keys), 0)
                - lax.broadcasted_iota(jnp.int32, (blk, n_keys), 1))
        ok = lax.bitcast_convert_type(dist, jnp.uint32) <= blk
        lo_k = lax.broadcasted_iota(jnp.int32, (n_keys, LANES), 1) < HEAD_DIM
        ms, ps, vhs = [], [], []
        for first_head in (True, False):
            qh = (jnp.where(lo, qb, 0.0) if first_head else jnp.where(lo, 0.0, qb)).astype(BF16)
            s = jnp.where(ok, _dot_nt(qh, kb), NEG_INF)
            m = jnp.max(s, axis=-1, keepdims=True)
            ms.append(m)
            ps.append(jnp.exp(s - m).astype(BF16))
            vhs.append((jnp.where(lo_k, vb, 1.0) if first_head else jnp.where(lo_k, 1.0, vb)).astype(BF16))
        return qrows, ms, ps, vhs

    def finish(br, qrows, ms, ps, vhs):
        rs = [_dot(p, vh) for p, vh in zip(ps, vhs)]
        m_scr[br, qrows, :] = jnp.where(lo, ms[0], ms[1])
        a_scr[br, qrows, :] = jnp.where(lo, rs[0], rs[1])
        l_scr[br, qrows, :] = jnp.where(lo, rs[1], rs[0])

    def run_branch(srcs, br, stride, n_classes, class_base, class_len):
        nb = class_len // blk

        def one(idx):
            n = idx // n_classes
            c = idx - n * n_classes
            base = class_base(c)
            if nb == 1:
                return attend(srcs, br, stride, base, base, blk, 0)
            first = n == 0
            q_start = base + n * (blk * stride)
            k_start = base + jnp.where(first, 0, n - 1) * (blk * stride)
            return attend(srcs, br, stride, q_start, k_start, 2 * blk, jnp.where(first, 0, blk))

        def body(trip, carry):
            held = [one(trip * ATT_UNROLL + g) for g in range(ATT_UNROLL)]
            for parts in held:
                finish(br, *parts)
            return carry

        lax.fori_loop(0, nb * n_classes // ATT_UNROLL, body, 0)

    natural = (q_ref, k_ref, v_ref)
    by_class = (qc_scr, kc_scr, vc_scr)
    run_branch(natural, 0, 1, 1, lambda c: 0, seq)
    run_branch(by_class, 1, 1, d1, lambda c: c * cls, cls)
    run_branch(by_class, 2, sub, d2, lambda c: (c // sub) * cls + c % sub, cls // sub)

    chunk = 256
    per_class = cls // chunk

    def combine(idx, carry):
        r = idx // per_class
        c = idx - r * per_class
        nat = pl.ds(r + c * (chunk * d1), chunk, stride=d1)
        byc = pl.ds(pl.multiple_of(r * cls + c * chunk, chunk), chunk)
        rows = (nat, byc, byc)
        m = [m_scr[b, rows[b], :] for b in range(3)]
        mx = jnp.maximum(jnp.maximum(m[0], m[1]), m[2])
        den = jnp.zeros((chunk, LANES), F32)
        num = jnp.zeros((chunk, LANES), F32)
        for b in range(3):
            w = jnp.exp(m[b] - mx)
            den = den + w * pltpu.roll(l_scr[b, rows[b], :], HEAD_DIM, axis=1)
            num = num + w * a_scr[b, rows[b], :]
        o_ref[nat, :] = num / den
        return carry

    lax.fori_loop(0, d1 * per_class, combine, 0)


def _attn_prompt(q, k, v, batch, seq):
    spec = pl.BlockSpec((seq, LANES), lambda b, hp: (b, hp))
    return pl.pallas_call(
        functools.partial(_attn_prompt_kernel, seq=seq),
        grid=(batch, ATT_WIDTH // LANES),
        in_specs=[spec, spec, spec],
        out_specs=spec,
        out_shape=jax.ShapeDtypeStruct((batch * seq, ATT_WIDTH), F32),
        scratch_shapes=[pltpu.VMEM((seq, LANES), F32)] * 3 + [pltpu.VMEM((3, seq, LANES), F32)] * 3,
        compiler_params=_params("parallel", "parallel"),
        name="attn_prompt",
    )(q, k, v)


def _attn_sample_kernel(q_ref, kn_ref, vn_ref, kc_ref, vc_ref, o_ref, ko_ref, vo_ref,
                        *, n_new, n_buf):
    for g in range(q_ref.shape[1]):
        _attn_sample_head(q_ref.at[0, g], kn_ref.at[0, g], vn_ref.at[0, g], kc_ref.at[0, g],
                          vc_ref.at[0, g], o_ref.at[0, g], ko_ref.at[0, g], vo_ref.at[0, g],
                          n_new=n_new, n_buf=n_buf)


def _attn_sample_head(q_ref, kn_ref, vn_ref, kc_ref, vc_ref, o_ref, ko_ref, vo_ref, *, n_new, n_buf):
    rows = q_ref.shape[0]
    q = q_ref[...].astype(BF16)
    kc = kc_ref[...]
    vc = vc_ref[...]
    kn = kn_ref[...]
    vn = vn_ref[...]
    s_c = _dot(q, kc.astype(BF16))
    s_n = _dot(q, kn.astype(BF16))
    tok_c = lax.broadcasted_iota(jnp.int32, (rows, n_buf), 0)
    dist_c = n_buf + tok_c - lax.broadcasted_iota(jnp.int32, (rows, n_buf), 1)
    tok_n = lax.broadcasted_iota(jnp.int32, (rows, LANES), 0)
    key_n = lax.broadcasted_iota(jnp.int32, (rows, LANES), 1) - (LANES - n_new)
    dist_n = tok_n - key_n
    parts = []
    for window, dil in ATT_BRANCHES:
        ok_c = jnp.where((dist_c & (dil - 1)) == 0, dist_c, window + 1) <= window
        ok_n = lax.bitcast_convert_type(
            jnp.where((dist_n & (dil - 1)) == 0, jnp.where(key_n >= 0, dist_n, -1), -1),
            jnp.uint32) <= window
        sc = jnp.where(ok_c, s_c, NEG_INF)
        sn = jnp.where(ok_n, s_n, NEG_INF)
        m = jnp.maximum(jnp.max(sc, axis=-1, keepdims=True), jnp.max(sn, axis=-1, keepdims=True))
        pc = jnp.exp(sc - m)
        pn = jnp.exp(sn - m)
        l = jnp.sum(pc, axis=-1, keepdims=True) + jnp.sum(pn, axis=-1, keepdims=True)
        acc = (_dot_nt(pc.astype(BF16), vc.astype(BF16))
               + _dot_nt(pn.astype(BF16), vn.astype(BF16)))
        parts.append((m, l, acc))
    mx = jnp.maximum(jnp.maximum(parts[0][0], parts[1][0]), parts[2][0])
    den = jnp.zeros((rows, 1), F32)
    num = jnp.zeros((rows, HEAD_DIM), F32)
    for m, l, acc in parts:
        w = jnp.exp(m - mx)
        den = den + w * l
        num = num + w * acc
    o_ref[...] = num / den
    fresh = lax.broadcasted_iota(jnp.int32, (HEAD_DIM, LANES), 1) >= LANES - n_new
    for c_val, n_val, out_ref in ((kc, kn, ko_ref), (vc, vn, vo_ref)):
        shifted = pltpu.roll(c_val, n_buf - n_new, axis=1)
        out_ref[:, :n_buf - LANES] = shifted[:, :n_buf - LANES]
        out_ref[:, n_buf - LANES:] = jnp.where(fresh, n_val, shifted[:, n_buf - LANES:])


def _attn_sample(q_pad, kn_t, vn_t, cache_kt, cache_vt, n_new):
    batch, heads, rows, _ = q_pad.shape
    n_buf = cache_kt.shape[3]
    idx = lambda b, h: (b, h, 0, 0)
    group = ATT_SAMPLE_HEADS
    buf_spec = pl.BlockSpec((1, group, HEAD_DIM, n_buf), idx)
    new_spec = pl.BlockSpec((1, group, HEAD_DIM, LANES), idx)
    q_spec = pl.BlockSpec((1, group, rows, HEAD_DIM), idx)
    buf_shape = jax.ShapeDtypeStruct((batch, heads, HEAD_DIM, n_buf), F32)
    return pl.pallas_call(
        functools.partial(_attn_sample_kernel, n_new=n_new, n_buf=n_buf),
        grid=(batch, heads // group),
        in_specs=[q_spec, new_spec, new_spec, buf_spec, buf_spec],
        out_specs=[q_spec, buf_spec, buf_spec],
        out_shape=[jax.ShapeDtypeStruct((batch, heads, rows, HEAD_DIM), F32), buf_shape, buf_shape],
        compiler_params=_params("parallel", "parallel"),
        name="attn_sample",
    )(q_pad, kn_t, vn_t, cache_kt, cache_vt)


def _hgrn_kernel(*refs, ts, n_valid, has_state, layer):
    if has_state:
        hq_ref, hf_ref, hi_ref, hg_ref, lbl_ref, nw_ref, s0_ref, o_ref, sfin_ref, st_scr = refs
    else:
        hq_ref, hf_ref, hi_ref, hg_ref, lbl_ref, nw_ref, o_ref, sfin_ref, st_scr = refs
    C = HGRN_CHUNK
    stretch = pl.program_id(1)

    @pl.when(stretch == 0)
    def _():
        for h in range(HGRN_HEADS):
            if has_state:
                st_scr[h] = s0_ref[0, h].T
            else:
                st_scr[h] = jnp.zeros((HGRN_DK, HGRN_DK), F32)

    lg = lbl_ref[...]
    ex = jnp.exp(lg - jnp.max(lg, axis=0, keepdims=True))
    lb_all = jnp.sum(ex[:layer + 1], axis=0, keepdims=True) / jnp.sum(ex, axis=0, keepdims=True)
    nw = nw_ref[...]
    rowi = lax.broadcasted_iota(jnp.int32, (C, HGRN_DK), 0)
    ones = jnp.ones((HGRN_DK, LANES), BF16)
    base = stretch * ts

    def one_head(h, r0, rows):
        cols = slice(h * HGRN_DK, (h + 1) * HGRN_DK)
        lb = lb_all[:, cols]
        f = lb + (1.0 - lb) * jax.nn.sigmoid(hf_ref[rows, cols])
        logf = jnp.log(f)
        kk = 1.0 - f
        if n_valid is not None:
            live = (rowi + (base + r0)) < n_valid
            logf = jnp.where(live, logf, 0.0)
            kk = jnp.where(live, kk, 0.0)
        q = jax.nn.silu(hq_ref[rows, cols])
        ii = hi_ref[rows, cols]
        b = logf
        sh = 1
        while sh < C:
            b = b + jnp.where(rowi >= sh, pltpu.roll(b, sh, axis=0), 0.0)
            sh *= 2
        st = st_scr[h]
        inter = _dot_nt((q * jnp.exp(b)).astype(BF16), st.astype(BF16))
        live_rows = [min(C, SUBLANES * (t // SUBLANES + 1)) for t in range(C)]
        prods = []
        for t, nr in enumerate(live_rows):
            e = jnp.exp(jnp.where(rowi[:nr] <= t, b[t:t + 1, :] - b[:nr], NEG_INF))
            prods.append(q[t:t + 1, :] * e * kk[:nr])
        a_rep = _dot(jnp.concatenate(prods, axis=0).astype(BF16), ones)
        intra, off = [], 0
        for nr in live_rows:
            intra.append(jnp.sum(a_rep[off:off + nr, :] * ii[:nr], axis=0, keepdims=True))
            off += nr
        o = inter + jnp.concatenate(intra, axis=0)
        b_last = b[C - 1:C, :]
        kdec = kk * jnp.exp(b_last - b)
        st_scr[h] = st * jnp.exp(b_last) + _dot_tn(ii.astype(BF16), kdec.astype(BF16))
        o_ref[rows, cols] = _rms(o, nw) * jax.nn.silu(hg_ref[rows, cols])

    def body(c, carry):
        r0 = pl.multiple_of(c * C, C)
        rows = pl.ds(r0, C)
        for h in range(HGRN_HEADS):
            one_head(h, r0, rows)
        return carry

    lax.fori_loop(0, ts // C, body, 0, unroll=min(4, ts // C))

    @pl.when(stretch == pl.num_programs(1) - 1)
    def _():
        for h in range(HGRN_HEADS):
            sfin_ref[0, h] = st_scr[h].T


def _hgrn(g4, lb_logits, norm_w, state, batch, seq, ts, n_valid, layer):
    has_state = state is not None
    nh = HGRN_HEADS
    per_seq = seq // ts
    col = lambda part: pl.BlockSpec((ts, HGRN_WIDTH), lambda b, s, part=part: (b * per_seq + s, part))
    st_spec = pl.BlockSpec((1, nh, HGRN_DK, HGRN_DK), lambda b, s: (b, 0, 0, 0))
    in_specs = [col(0), col(1), col(2), col(3),
                pl.BlockSpec(lb_logits.shape, lambda b, s: (0, 0)),
                pl.BlockSpec((1, HGRN_DK), lambda b, s: (0, 0))]
    args = [g4, g4, g4, g4, lb_logits, norm_w]
    if has_state:
        in_specs.append(st_spec)
        args.append(state)
    return pl.pallas_call(
        functools.partial(_hgrn_kernel, ts=ts, n_valid=n_valid, has_state=has_state, layer=layer),
        grid=(batch, per_seq),
        in_specs=in_specs,
        out_specs=[pl.BlockSpec((ts, HGRN_WIDTH), lambda b, s: (b * per_seq + s, 0)), st_spec],
        out_shape=[jax.ShapeDtypeStruct((batch * seq, HGRN_WIDTH), F32),
                   jax.ShapeDtypeStruct((batch, nh, HGRN_DK, HGRN_DK), F32)],
        scratch_shapes=[pltpu.VMEM((nh, HGRN_DK, HGRN_DK), F32)],
        compiler_params=_params("parallel", "arbitrary"),
        name="hgrn",
    )(*args)


def _out_proj_kernel(x_ref, att_ref, og_ref, wo_ref, nw_ref, wq_ref, h_ref, xn_ref, qp_ref):
    h = (x_ref[...]
         + _dot(att_ref[...].astype(BF16), wo_ref[:ATT_WIDTH, :])
         + _dot(og_ref[...].astype(BF16), wo_ref[ATT_WIDTH:, :]))
    h_ref[...] = h
    xn = _rms(h, nw_ref[...]).astype(BF16)
    xn_ref[...] = xn
    qp_ref[...] = _dot(xn, wq_ref[...])


def _out_proj(x, att, og, w_out_b, norm_w, wq_b, tm):
    n = x.shape[0]
    qcols = wq_b.shape[1]
    row = lambda i: (i, 0)
    fixed = lambda i: (0, 0)
    return pl.pallas_call(
        _out_proj_kernel,
        grid=(n // tm,),
        in_specs=[pl.BlockSpec((tm, D_MODEL), row),
                  pl.BlockSpec((tm, ATT_WIDTH), row),
                  pl.BlockSpec((tm, HGRN_WIDTH), row),
                  pl.BlockSpec((ATT_WIDTH + HGRN_WIDTH, D_MODEL), fixed),
                  pl.BlockSpec((1, D_MODEL), fixed),
                  pl.BlockSpec((D_MODEL, qcols), fixed)],
        out_specs=[pl.BlockSpec((tm, D_MODEL), row),
                   pl.BlockSpec((tm, D_MODEL), row),
                   pl.BlockSpec((tm, qcols), row)],
        out_shape=[jax.ShapeDtypeStruct((n, D_MODEL), F32),
                   jax.ShapeDtypeStruct((n, D_MODEL), BF16),
                   jax.ShapeDtypeStruct((n, qcols), F32)],
        compiler_params=_params("parallel"),
        name="out_proj",
    )(x, att, og, w_out_b, norm_w, wq_b)


def _bitonic_merge(v):
    n = len(v)
    j = n // 2
    while j >= 1:
        for i in range(n):
            l = i ^ j
            if l > i:
                v[i], v[l] = jnp.maximum(v[i], v[l]), jnp.minimum(v[i], v[l])
        j //= 2
    return v


def _top_values(s, count):
    assert s.shape[0] == count * SUBLANES and count & (count - 1) == 0
    v = [s[i * SUBLANES:(i + 1) * SUBLANES] for i in range(count)]
    k = 2
    while k <= count:
        j = k // 2
        while j >= 1:
            for i in range(count):
                l = i ^ j
                if l > i:
                    hi, lo = jnp.maximum(v[i], v[l]), jnp.minimum(v[i], v[l])
                    v[i], v[l] = (hi, lo) if (i & k) == 0 else (lo, hi)
            j //= 2
        k *= 2
    shift = SUBLANES // 2
    while shift >= 1:
        other = [pltpu.roll(v[count - 1 - i], shift, axis=0) for i in range(count)]
        v = _bitonic_merge([jnp.maximum(a, b) for a, b in zip(v, other)])
        shift //= 2
    return jnp.concatenate([a[0:1] for a in v], axis=0)


def _peer_scores_kernel(qp_ref, keys_ref, s2_ref, th_ref, e1_ref, e2_ref, th_scr, e1_scr):
    tn = qp_ref.shape[0]
    k = PEER_TOPK
    half = PEER_DKEY // 2
    for h in range(PEER_HEADS):
        q1 = qp_ref[:, h * PEER_DKEY:h * PEER_DKEY + half].astype(BF16)
        q2 = qp_ref[:, h * PEER_DKEY + half:(h + 1) * PEER_DKEY].astype(BF16)
        s1 = _dot_nt(keys_ref[h, 0], q1)
        s2 = _dot_nt(keys_ref[h, 1], q2)
        t1 = _top_values(s1, k)
        t2 = _top_values(s2, k)
        half_k = k // 2
        cands = []
        for i in range(half_k - 1):
            cands += [t1[i:i + 1, :] + t2[:half_k, :], t1[i:i + 1, :] + t2[half_k:, :]]
        cands += [t1[half_k - 1:half_k, :] + t2[:half_k, :], t1[half_k:, :] + t2[0:1, :]]
        chosen = _top_values(jnp.concatenate(cands, axis=0), k)
        tau = chosen[k - 1:k, :]
        z = jnp.sum(jnp.exp(chosen - chosen[0:1, :]), axis=0, keepdims=True)
        best2 = t2[0:1, :]
        th = jnp.where(s1 + best2 >= tau, best2, jnp.inf)
        for i in range(k // 2):
            t1i = t1[i:i + 1, :]
            thr = jnp.full((1, tn), jnp.inf, F32)
            for j in range(k // (i + 1)):
                t2j = t2[j:j + 1, :]
                thr = jnp.where(t1i + t2j >= tau, t2j, thr)
            th = jnp.where(s1 == t1i, thr, th)
        s2_ref[h] = s2
        th_scr[h] = th
        e1_scr[h] = jnp.exp(s1 - t1[0:1, :])
        e2_ref[h] = jnp.exp(s2 - t2[0:1, :]) * (0.5 / z)
    th_ref[...] = jnp.transpose(th_scr[...], (1, 0, 2))
    e1_ref[...] = jnp.transpose(e1_scr[...], (1, 0, 2))


def _peer_scores(qp, keys_b, tn):
    n = qp.shape[0]
    by_head = pl.BlockSpec((PEER_HEADS, N_KEYS, tn), lambda i: (0, 0, i))
    by_key = pl.BlockSpec((N_KEYS, PEER_HEADS, tn), lambda i: (0, 0, i))
    head_shape = jax.ShapeDtypeStruct((PEER_HEADS, N_KEYS, n), F32)
    key_shape = jax.ShapeDtypeStruct((N_KEYS, PEER_HEADS, n), F32)
    return pl.pallas_call(
        _peer_scores_kernel,
        grid=(n // tn,),
        in_specs=[pl.BlockSpec((tn, PEER_HEADS * PEER_DKEY), lambda i: (i, 0)),
                  pl.BlockSpec((PEER_HEADS, 2, N_KEYS, PEER_DKEY // 2), lambda i: (0, 0, 0, 0))],
        out_specs=[by_head, by_key, by_key, by_head],
        out_shape=[head_shape, key_shape, key_shape, head_shape],
        scratch_shapes=[pltpu.VMEM((PEER_HEADS, N_KEYS, tn), F32)] * 2,
        compiler_params=_params("parallel"),
        name="peer_scores",
    )(qp, keys_b)


def _peer_kernel(x_ref, h_ref, u_ref, vt_ref, s2_in, th_ref, e1_ref, e2_in, nfw_ref,
                 o_ref, acc_scr, z_scr, w_scr, c0_scr, c1_scr, se_ref, *, eb, tn, nblk):
    j = pl.program_id(1)
    slabs = eb // N_KEYS

    def coefficients(c_ref):
        z_scr[...] = _dot_nt(u_ref[...], x_ref[...])
        zeros = [jnp.zeros((SUBLANES, LANES), F32)] * 2
        tile = 0
        for al in range(slabs):
            a = j * slabs + al
            for tc in range(tn // LANES):
                lanes = slice(tc * LANES, (tc + 1) * LANES)
                ths = [th_ref[a, h:h + 1, lanes] for h in range(PEER_HEADS)]
                e1s = [e1_ref[a, h:h + 1, lanes] for h in range(PEER_HEADS)]
                zero = zeros[tile % 2]
                for v in range(N_KEYS // SUBLANES):
                    brow = slice(v * SUBLANES, (v + 1) * SUBLANES)
                    w = zero
                    for h in range(PEER_HEADS):
                        w = w + jnp.where(se_ref[tc, h, 0, brow, :] >= ths[h],
                                          se_ref[tc, h, 1, brow, :], 0.0) * e1s[h]
                    if v == 0:
                        zeros[tile % 2] = pltpu.roll(pltpu.roll(w * 0.0, 1, axis=1), 1, axis=1)
                    w_scr[tc, al * N_KEYS + v * SUBLANES:al * N_KEYS + (v + 1) * SUBLANES, :] = w
                tile += 1
        rows = 2 * SUBLANES
        for r in range(eb // rows):
            erow = slice(r * rows, (r + 1) * rows)
            z = z_scr[erow, :]
            w = jnp.concatenate([w_scr[tc, erow, :] for tc in range(tn // LANES)], axis=1)
            c_ref[erow, :] = (w * (z * (1.0 + lax.erf(z * math.sqrt(0.5))))).astype(BF16)

    def accumulate(c_ref):
        acc_scr[...] += _dot(vt_ref[...], c_ref[...])

    odd = (j % 2) == 1
    inner = jnp.logical_and(j > 0, j < nblk)

    @pl.when(j == 0)
    def _():
        for tc in range(tn // LANES):
            lanes = slice(tc * LANES, (tc + 1) * LANES)
            for h in range(PEER_HEADS):
                se_ref[tc, h, 0, :N_KEYS, :] = s2_in[h, :, lanes]
                se_ref[tc, h, 1, :N_KEYS, :] = e2_in[h, :, lanes]
        acc_scr[...] = jnp.zeros_like(acc_scr)
        coefficients(c0_scr)

    @pl.when(jnp.logical_and(inner, odd))
    def _():
        accumulate(c0_scr)
        coefficients(c1_scr)

    @pl.when(jnp.logical_and(inner, jnp.logical_not(odd)))
    def _():
        accumulate(c1_scr)
        coefficients(c0_scr)

    @pl.when(j == nblk)
    def _():
        accumulate(c1_scr if (nblk - 1) % 2 == 1 else c0_scr)
        hh = h_ref[...] + acc_scr[...].T
        o_ref[...] = _rms(hh, nfw_ref[...])


def _peer(xn, h, u_b, vt_b, s2, th, e1, e2, nf_w, tn, eb):
    n = xn.shape[0]
    nblk = u_b.shape[0] // eb
    tok = pl.BlockSpec((tn, D_MODEL), lambda i, j: (i, 0))
    by_head = pl.BlockSpec((PEER_HEADS, N_KEYS, tn), lambda i, j: (0, 0, i))
    by_key = pl.BlockSpec((N_KEYS, PEER_HEADS, tn), lambda i, j: (0, 0, i))
    return pl.pallas_call(
        functools.partial(_peer_kernel, eb=eb, tn=tn, nblk=nblk),
        grid=(n // tn, nblk + 1),
        in_specs=[tok, tok,
                  pl.BlockSpec((eb, D_MODEL), lambda i, j: (jnp.minimum(j, nblk - 1), 0)),
                  pl.BlockSpec((D_MODEL, eb), lambda i, j: (0, jnp.maximum(j - 1, 0))),
                  by_head, by_key, by_key, by_head,
                  pl.BlockSpec((1, D_MODEL), lambda i, j: (0, 0))],
        out_specs=tok,
        out_shape=jax.ShapeDtypeStruct((n, D_MODEL), F32),
        scratch_shapes=[pltpu.VMEM((D_MODEL, tn), F32),
                        pltpu.VMEM((eb, tn), F32),
                        pltpu.VMEM((tn // LANES, eb, LANES), F32),
                        pltpu.VMEM((eb, tn), BF16),
                        pltpu.VMEM((eb, tn), BF16),
                        pltpu.VMEM((tn // LANES, PEER_HEADS, 2, N_KEYS + PEER_SLAB_PAD, LANES), F32)],
        compiler_params=_params("parallel", "arbitrary"),
        name="peer_experts",
    )(xn, h, u_b, vt_b, s2, th, e1, e2, nf_w)


def _tail(x, att, og, w, tm, tn, eb):
    h, xn, qp = _out_proj(x, att, og, w["w_out"], w["norm2"], w["wq"], tm)
    s2, th, e1, e2 = _peer_scores(qp, w["keys"], min(tn, 256))
    return _peer(xn, h, w["u"], w["vt"], s2, th, e1, e2, w["norm_f"], tn, eb)


def _window_view(t):
    return jnp.transpose(t, (0, 2, 3, 1))


def _window_unview(t):
    return jnp.transpose(t, (0, 3, 1, 2))


def kernel(x_prompt, x_sample, cache_k_win, cache_v_win, state_hgrn, norm1_w, w_in,
           hgrn_norm_w, hgrn_lb_logits, w_out, norm2_w, peer_wq, peer_sub_keys,
           peer_u, peer_v, norm_f_w):
    layer = 0
    batch, seq, _ = x_prompt.shape
    dec_batch, dec_seq, _ = x_sample.shape
    w_in_b = w_in[layer].astype(BF16)
    w = {
        "norm1": norm1_w[layer][None, :],
        "w_in": w_in_b,
        "w_kvt": w_in_b[:, ATT_WIDTH:3 * ATT_WIDTH].T,
        "hgrn_norm": hgrn_norm_w[layer][None, :],
        "w_out": w_out[layer].astype(BF16),
        "norm2": norm2_w[layer][None, :],
        "wq": peer_wq[layer].astype(BF16),
        "keys": peer_sub_keys[layer].astype(BF16),
        "u": peer_u[layer].astype(BF16),
        "vt": peer_v[layer].astype(BF16).T,
        "norm_f": norm_f_w[None, :],
    }
    heads = (ATT_HEADS, HEAD_DIM)

    xp = x_prompt.reshape(batch * seq, D_MODEL)
    q, k, v, g4, kt, vt = _in_proj(xp, w["norm1"], w["w_in"], w["w_kvt"], 512, seq)
    att = _attn_prompt(q, k, v, batch, seq)
    og, s_prompt = _hgrn(g4, hgrn_lb_logits, w["hgrn_norm"], None, batch, seq, seq // 2, None, layer)
    y_prompt = _tail(xp, att, og, w, 512, 512, PEER_EXPERT_BLOCK).reshape(batch, seq, D_MODEL)
    keep = min(max(wd for wd, _ in ATT_BRANCHES), seq)
    k_win_prompt = _window_unview(kt.reshape(batch, *heads, seq)[..., seq - keep:])
    v_win_prompt = _window_unview(vt.reshape(batch, *heads, seq)[..., seq - keep:])

    tms = dec_batch * dec_seq
    xs = x_sample.reshape(tms, D_MODEL)
    qs, _, _, g4s, kts, vts = _in_proj(xs, w["norm1"], w["w_in"], w["w_kvt"], tms, tms)
    rows = SUBLANES
    q_pad = jnp.pad(jnp.transpose(qs.reshape(dec_batch, dec_seq, *heads), (0, 2, 1, 3)),
                    ((0, 0), (0, 0), (0, rows - dec_seq), (0, 0)))
    new_t = lambda t: jnp.pad(
        jnp.transpose(t.reshape(*heads, dec_batch, dec_seq), (2, 0, 1, 3)),
        ((0, 0), (0, 0), (0, 0), (LANES - dec_seq, 0)))
    att_s, k_buf, v_buf = _attn_sample(q_pad, new_t(kts), new_t(vts),
                                       _window_view(cache_k_win[layer]),
                                       _window_view(cache_v_win[layer]), dec_seq)
    att_s = jnp.transpose(att_s[:, :, :dec_seq], (0, 2, 1, 3)).reshape(tms, ATT_WIDTH)
    seq_pad = HGRN_CHUNK
    g4p = jnp.pad(g4s.reshape(dec_batch, dec_seq, 4 * HGRN_WIDTH),
                  ((0, 0), (0, seq_pad - dec_seq), (0, 0))).reshape(dec_batch * seq_pad, 4 * HGRN_WIDTH)
    ogp, s_sample = _hgrn(g4p, hgrn_lb_logits, w["hgrn_norm"], state_hgrn[layer],
                          dec_batch, seq_pad, seq_pad, dec_seq, layer)
    og_s = ogp.reshape(dec_batch, seq_pad, HGRN_WIDTH)[:, :dec_seq].reshape(tms, HGRN_WIDTH)
    y_sample = _tail(xs, att_s, og_s, w, tms, tms, PEER_EXPERT_BLOCK).reshape(dec_batch, dec_seq, D_MODEL)

    return (y_prompt, y_sample,
            k_win_prompt[None], v_win_prompt[None],
            s_prompt[None],
            _window_unview(k_buf)[None], _window_unview(v_buf)[None],
            s_sample.astype(state_hgrn.dtype)[None])
```
